```python
import jax, jax.numpy as jnp
from jax import lax
import numpy as np

D_MODEL = 2048
BATCH = 8
SEQ = 2048
DEPTH = 1
DEC_BATCH = 128
DEC_SEQ = 8
PAST_LEN = 2048
PAGE_SIZE = 128

HEAD_DIM = 128
ATTN_HEADS = 8
DIL_GROUPS = ((128, 1), (512, 4), (2048, 16))
N_DIL = len(DIL_GROUPS)
ATTN_WIDTH = ATTN_HEADS * HEAD_DIM
QKV_WIDTH = N_DIL * ATTN_WIDTH
CHUNK = 128
GMLP_GROUPS = 8
GMLP_GROUP_DIM = 128
GMLP_WIDTH = GMLP_GROUPS * GMLP_GROUP_DIM
ROPE_THETA = 10000.0
LN_EPS = 1e-5
DN_ALPHA = float(2 * DEPTH) ** 0.25
DN_BETA = float(8 * DEPTH) ** -0.25
SPLIT_SIZES = (QKV_WIDTH, QKV_WIDTH, QKV_WIDTH, ATTN_WIDTH, GMLP_WIDTH, GMLP_WIDTH, GMLP_WIDTH, D_MODEL, D_MODEL)
SPLIT_POINTS = tuple(int(s) for s in np.cumsum(SPLIT_SIZES)[:-1])
IN_WIDTH = int(sum(SPLIT_SIZES))

kernel_name = "gated_dilated_attn_gmlp_deepnorm_step"


def layer_norm(x, g, b):
    xf = x.astype(jnp.float32)
    mu = jnp.mean(xf, axis=-1, keepdims=True)
    var = jnp.mean(jnp.square(xf - mu), axis=-1, keepdims=True)
    return ((xf - mu) * lax.rsqrt(var + LN_EPS) * g.astype(jnp.float32) + b.astype(jnp.float32)).astype(x.dtype)


def rope(x, pos):
    half = x.shape[-1] // 2
    inv = ROPE_THETA ** (-jnp.arange(0, half, dtype=jnp.float32) * 2.0 / x.shape[-1])
    ang = pos.astype(jnp.float32)[:, None] * inv[None, :]
    bshape = (pos.shape[0],) + (1,) * (x.ndim - 3) + (half,)
    cos = jnp.cos(ang).reshape(bshape)
    sin = jnp.sin(ang).reshape(bshape)
    xf = x.astype(jnp.float32)
    x1, x2 = xf[..., :half], xf[..., half:]
    return jnp.concatenate([x1 * cos - x2 * sin, x2 * cos + x1 * sin], axis=-1).astype(x.dtype)


def project_in(x, w_in, b_in):
    h = x @ w_in + b_in
    q, k, v, z_b, u, v_a, z_a, g_a, g_b = jnp.split(h, SPLIT_POINTS, axis=-1)
    shp = x.shape[:-1] + (N_DIL, ATTN_HEADS, HEAD_DIM)
    return q.reshape(shp), k.reshape(shp), v.reshape(shp), z_b, u, v_a, z_a, g_a, g_b


def dilated_attn_prompt(q, k, v, window, dil):
    B, S, H, dh = q.shape
    band = window // dil
    L = S // dil
    nb = -(-L // band)
    Lp = nb * band

    def sub(x):
        return jnp.pad(x.reshape(B, L, dil, H, dh), ((0, 0), (0, Lp - L), (0, 0), (0, 0), (0, 0)))

    def win(x):
        x = jnp.pad(x, ((0, 0), (band, 0), (0, 0), (0, 0), (0, 0))).reshape(B, nb + 1, band, dil, H, dh)
        return jnp.concatenate([x[:, :-1], x[:, 1:]], axis=2)

    qb = sub(q).reshape(B, nb, band, dil, H, dh).astype(jnp.float32)
    kw = win(sub(k)).astype(jnp.float32)
    vw = win(sub(v)).astype(jnp.float32)
    s = jnp.einsum('bnqrhe,bnkrhe->bnrhqk', qb, kw) * (dh ** -0.5)
    qi = jnp.arange(nb)[:, None, None] * band + jnp.arange(band)[None, :, None]
    ki = jnp.arange(nb)[:, None, None] * band - band + jnp.arange(2 * band)[None, None, :]
    mask = (ki >= 0) & (qi - ki >= 0) & (qi - ki <= band)
    s = jnp.where(mask[None, :, None, None], s, -jnp.inf)
    m = jnp.max(s, axis=-1, keepdims=True)
    p = jnp.exp(s - m)
    den = jnp.sum(p, axis=-1)
    lse = m[..., 0] + jnp.log(den)
    o = jnp.einsum('bnrhqk,bnkrhe->bnqrhe', p, vw) / jnp.transpose(den, (0, 1, 4, 2, 3))[..., None]
    o = o.reshape(B, Lp, dil, H, dh)[:, :L].reshape(B, S, H, dh)
    lse = jnp.transpose(lse, (0, 1, 4, 2, 3)).reshape(B, Lp, dil, H)[:, :L].reshape(B, S, H)
    return o, lse


def dilated_attn_sample(q, k_new, v_new, k_cache, v_cache, window, dil):
    T, dh = q.shape[1], q.shape[-1]
    band = window // dil
    Wc = k_cache.shape[1]
    k_all = jnp.concatenate([k_cache, k_new], axis=1)
    v_all = jnp.concatenate([v_cache, v_new], axis=1)
    idx = Wc + jnp.arange(T)[:, None] - dil * jnp.arange(band + 1)[None, :]
    valid = idx >= 0
    idx = jnp.maximum(idx, 0)
    kg = k_all[:, idx].astype(jnp.float32)
    vg = v_all[:, idx].astype(jnp.float32)
    s = jnp.einsum('bthe,btjhe->bthj', q.astype(jnp.float32), kg) * (dh ** -0.5)
    s = jnp.where(valid[None, :, None, :], s, -jnp.inf)
    m = jnp.max(s, axis=-1, keepdims=True)
    p = jnp.exp(s - m)
    den = jnp.sum(p, axis=-1)
    o = jnp.einsum('bthj,btjhe->bthe', p, vg) / den[..., None]
    return o, m[..., 0] + jnp.log(den)


def merge_dilation_groups(outs):
    o = jnp.stack([o_g for o_g, _ in outs])
    lse = jnp.stack([l_g for _, l_g in outs])
    wts = jax.nn.softmax(lse, axis=0)
    return jnp.einsum('gbsh,gbshe->bshe', wts, o)


def gmlp_branch(u, v, z, ln_g, ln_b, w_s, b_s):
    Bn, S, _ = u.shape
    rows = min(CHUNK, S)
    u = jax.nn.gelu(u, approximate=False)
    v = layer_norm(jax.nn.gelu(v, approximate=False), ln_g, ln_b)
    vr = v.reshape(Bn, S // rows, rows, GMLP_GROUPS, GMLP_GROUP_DIM)
    w_c = jnp.tril(w_s[:, :rows, :rows])
    mixed = jnp.einsum('gij,bcjgd->bcigd', w_c, vr) + jnp.swapaxes(b_s[:, :rows], 0, 1)[:, :, None]
    mixed = mixed.reshape(Bn, S, GMLP_WIDTH)
    return u * mixed * jax.nn.silu(z), v


def mix_and_norm(x, y_a, o_b, z_b, g_a, g_b, w_o_a, w_o_b, w_out, ln_g, ln_b):
    y_b = o_b.reshape(o_b.shape[:2] + (ATTN_WIDTH,)).astype(x.dtype) * jax.nn.silu(z_b)
    merged = jax.nn.sigmoid(g_a) * (y_a @ w_o_a) + jax.nn.sigmoid(g_b) * (y_b @ w_o_b)
    return layer_norm(DN_ALPHA * x + merged @ w_out, ln_g, ln_b)


def setup_inputs(seed: int = 0) -> dict:
    key = jax.random.key(seed)
    ks = jax.random.split(key, 24)
    f32 = jnp.float32

    def nrm(k, shape, scale):
        return jax.random.normal(k, shape, f32) * scale

    inp = {}
    inp['x_prompt'] = nrm(ks[0], (BATCH, SEQ, D_MODEL), 1.0)
    inp['x_sample'] = nrm(ks[1], (DEC_BATCH, DEC_SEQ, D_MODEL), 1.0)
    for g, (win, _) in enumerate(DIL_GROUPS):
        keep = min(win, PAST_LEN)
        shp = (DEPTH, DEC_BATCH, keep, ATTN_HEADS, HEAD_DIM)
        inp['cache_k_w%d' % win] = nrm(ks[2 + 2 * g], shp, 1.0)
        inp['cache_v_w%d' % win] = nrm(ks[3 + 2 * g], shp, 1.0)
    inp['w_in'] = nrm(ks[8], (DEPTH, D_MODEL, IN_WIDTH), D_MODEL ** -0.5)
    inp['b_in'] = nrm(ks[9], (DEPTH, IN_WIDTH), 0.01)
    inp['w_s'] = nrm(ks[10], (DEPTH, GMLP_GROUPS, CHUNK, CHUNK), CHUNK ** -0.5)
    inp['b_s'] = 1.0 + nrm(ks[11], (DEPTH, GMLP_GROUPS, CHUNK), 0.1)
    inp['ln_v_g'] = 1.0 + nrm(ks[12], (DEPTH, GMLP_WIDTH), 0.05)
    inp['ln_v_b'] = nrm(ks[13], (DEPTH, GMLP_WIDTH), 0.02)
    inp['w_o_a'] = nrm(ks[14], (DEPTH, GMLP_WIDTH, D_MODEL), DN_BETA * GMLP_WIDTH ** -0.5)
    inp['w_o_b'] = nrm(ks[15], (DEPTH, ATTN_WIDTH, D_MODEL), DN_BETA * ATTN_WIDTH ** -0.5)
    inp['w_out'] = nrm(ks[16], (DEPTH, D_MODEL, D_MODEL), DN_BETA * D_MODEL ** -0.5)
    inp['ln_g'] = 1.0 + nrm(ks[17], (DEPTH, D_MODEL), 0.05)
    inp['ln_b'] = nrm(ks[18], (DEPTH, D_MODEL), 0.02)
    return inp


def reference(x_prompt, x_sample, cache_k_w128, cache_v_w128, cache_k_w512, cache_v_w512,
              cache_k_w2048, cache_v_w2048, w_in, b_in, w_s, b_s, ln_v_g, ln_v_b,
              w_o_a, w_o_b, w_out, ln_g, ln_b):
    S = x_prompt.shape[1]
    T = x_sample.shape[1]
    pos_p = jnp.arange(S, dtype=jnp.int32)
    pos_s = PAST_LEN + jnp.arange(T, dtype=jnp.int32)
    caches_k = (cache_k_w128, cache_k_w512, cache_k_w2048)
    caches_v = (cache_v_w128, cache_v_w512, cache_v_w2048)
    new_pk = [[] for _ in DIL_GROUPS]
    new_pv = [[] for _ in DIL_GROUPS]
    new_sk = [[] for _ in DIL_GROUPS]
    new_sv = [[] for _ in DIL_GROUPS]
    new_gv = []
    xp, xs = x_prompt, x_sample
    for l in range(DEPTH):
        q, k, v, z_b, u, v_a, z_a, g_a, g_b = project_in(xp, w_in[l], b_in[l])
        q = rope(q, pos_p)
        k = rope(k, pos_p)
        outs = []
        for g, (win, dil) in enumerate(DIL_GROUPS):
            outs.append(dilated_attn_prompt(q[:, :, g], k[:, :, g], v[:, :, g], win, dil))
            keep = min(win, S)
            new_pk[g].append(k[:, S - keep:, g])
            new_pv[g].append(v[:, S - keep:, g])
        o_b = merge_dilation_groups(outs)
        y_a, _ = gmlp_branch(u, v_a, z_a, ln_v_g[l], ln_v_b[l], w_s[l], b_s[l])
        xp = mix_and_norm(xp, y_a, o_b, z_b, g_a, g_b, w_o_a[l], w_o_b[l], w_out[l], ln_g[l], ln_b[l])

        q, k, v, z_b, u, v_a, z_a, g_a, g_b = project_in(xs, w_in[l], b_in[l])
        q = rope(q, pos_s)
        k = rope(k, pos_s)
        outs = []
        for g, (win, dil) in enumerate(DIL_GROUPS):
            outs.append(dilated_attn_sample(q[:, :, g], k[:, :, g], v[:, :, g],
                                            caches_k[g][l], caches_v[g][l], win, dil))
            new_sk[g].append(k[:, :, g])
            new_sv[g].append(v[:, :, g])
        o_b = merge_dilation_groups(outs)
        y_a, v_rows = gmlp_branch(u, v_a, z_a, ln_v_g[l], ln_v_b[l], w_s[l], b_s[l])
        new_gv.append(v_rows)
        xs = mix_and_norm(xs, y_a, o_b, z_b, g_a, g_b, w_o_a[l], w_o_b[l], w_out[l], ln_g[l], ln_b[l])

    new_k_w128_prompt = jnp.stack(new_pk[0])
    new_v_w128_prompt = jnp.stack(new_pv[0])
    new_k_w512_prompt = jnp.stack(new_pk[1])
    new_v_w512_prompt = jnp.stack(new_pv[1])
    new_k_w2048_prompt = jnp.stack(new_pk[2])
    new_v_w2048_prompt = jnp.stack(new_pv[2])
    new_k_w128_sample = jnp.stack(new_sk[0])
    new_v_w128_sample = jnp.stack(new_sv[0])
    new_k_w512_sample = jnp.stack(new_sk[1])
    new_v_w512_sample = jnp.stack(new_sv[1])
    new_k_w2048_sample = jnp.stack(new_sk[2])
    new_v_w2048_sample = jnp.stack(new_sv[2])
    new_gmlp_v_sample = jnp.stack(new_gv)
    return (xp, xs,
            new_k_w128_prompt, new_v_w128_prompt, new_k_w512_prompt, new_v_w512_prompt,
            new_k_w2048_prompt, new_v_w2048_prompt,
            new_k_w128_sample, new_v_w128_sample, new_k_w512_sample, new_v_w512_sample,
            new_k_w2048_sample, new_v_w2048_sample,
            new_gmlp_v_sample)
```

```python
import functools

import numpy as np
import jax
import jax.numpy as jnp
from jax import lax
from jax.experimental import pallas as pl
from jax.experimental.pallas import tpu as pltpu

F32 = jnp.float32
BF16 = jnp.bfloat16

D_MODEL = 2048
HEAD_DIM = 128
HEADS = 8
DIL_GROUPS = ((128, 1), (512, 4), (2048, 16))
N_DIL = len(DIL_GROUPS)
WIDTH = HEADS * HEAD_DIM
CHUNK = 128
ROPE_THETA = 10000.0
LN_EPS = 1e-5
N_COL_TILES = 17
J_K0, J_V0, J_ZB, J_U, J_VA, J_ZA, J_GA, J_GB = 3, 6, 9, 10, 11, 12, 13, 15
EPILOGUE_ROWS = 64
LSE_REP_LOG2 = 4
LSE_REP = 1 << LSE_REP_LOG2
VMEM_LIMIT = 56 * 1024 * 1024


def _gelu(x):
    return 0.5 * x * (1.0 + lax.erf(x * np.float32(np.sqrt(0.5))))


def _sigmoid(x):
    return 1.0 / (1.0 + jnp.exp(-x))


def _silu(x):
    return x * _sigmoid(x)


def _layer_norm_rows(x, g, b):
    mu = jnp.mean(x, axis=-1, keepdims=True)
    xc = x - mu
    var = jnp.mean(xc * xc, axis=-1, keepdims=True)
    return xc * lax.rsqrt(var + LN_EPS) * g + b


def _inproj_body(*refs, tm, permuted, f32_specs):
    if permuted:
        xn_ref, x4_ref, x16_ref = refs[:3]
        refs = refs[3:]
    else:
        xn_ref = refs[0]
        refs = refs[1:]
    w_ref, b_ref, cos_ref, sin_ref, lng_ref, lnb_ref, h_ref = refs[:7]
    f32_refs = refs[7:7 + len(f32_specs)]
    acc_ref = refs[-1]

    j = pl.program_id(0)
    i = pl.program_id(1)
    tiles_per_batch = 2048 // tm

    if permuted:
        perm = jnp.where(j < 9, j % 3, 0)

        @pl.when(perm == 0)
        def _():
            acc_ref[...] = jnp.dot(xn_ref[...], w_ref[...], preferred_element_type=F32)

        @pl.when(perm == 1)
        def _():
            acc_ref[...] = jnp.dot(x4_ref[...], w_ref[...], preferred_element_type=F32)

        @pl.when(perm == 2)
        def _():
            xcat = jnp.concatenate(
                [x16_ref[:, r * D_MODEL:(r + 1) * D_MODEL] for r in range(4)], axis=0)
            acc_ref[...] = jnp.dot(xcat, w_ref[...], preferred_element_type=F32)
    else:
        acc_ref[...] = jnp.dot(xn_ref[...], w_ref[...], preferred_element_type=F32)

    def run_epilogue(fn):
        def step(c, carry):
            r0 = pl.multiple_of(c * EPILOGUE_ROWS, EPILOGUE_ROWS)
            a = acc_ref[pl.ds(r0, EPILOGUE_ROWS), :] + b_ref[...]
            res = fn(a, r0)
            acc_ref[pl.ds(r0, EPILOGUE_ROWS), :] = res
            h_ref[pl.ds(r0, EPILOGUE_ROWS), :] = res.astype(BF16)
            return carry
        lax.fori_loop(0, tm // EPILOGUE_ROWS, step, 0)

    def rope(a, r0):
        cos = cos_ref[pl.ds(r0, EPILOGUE_ROWS), :]
        sin = sin_ref[pl.ds(r0, EPILOGUE_ROWS), :]
        parts = []
        for h in range(HEADS):
            ah = a[:, h * HEAD_DIM:(h + 1) * HEAD_DIM]
            parts.append(ah * cos + pltpu.roll(ah, HEAD_DIM // 2, 1) * sin)
        return jnp.concatenate(parts, axis=1)

    @pl.when(j < J_V0)
    def _():
        run_epilogue(rope)

    @pl.when((j >= J_V0) & (j < J_ZB))
    def _():
        run_epilogue(lambda a, r0: a)

    @pl.when((j == J_ZB) | (j == J_ZA))
    def _():
        run_epilogue(lambda a, r0: _silu(a))

    @pl.when(j == J_U)
    def _():
        run_epilogue(lambda a, r0: _gelu(a))

    @pl.when(j == J_VA)
    def _():
        run_epilogue(lambda a, r0: _layer_norm_rows(_gelu(a), lng_ref[...], lnb_ref[...]))

    @pl.when(j >= J_GA)
    def _():
        run_epilogue(lambda a, r0: _sigmoid(a))

    for (j_own, kind), o_ref in zip(f32_specs, f32_refs):
        if kind == 'all':
            @pl.when(j == j_own)
            def _(o_ref=o_ref):
                o_ref[...] = acc_ref[...]
        elif kind == 'tail':
            @pl.when((j == j_own) & (i % tiles_per_batch == tiles_per_batch - 1))
            def _(o_ref=o_ref):
                o_ref[...] = acc_ref[tm - 128:, :]
        elif kind == 'tail4':
            @pl.when(j == j_own)
            def _(o_ref=o_ref):
                o_ref[...] = acc_ref[tm - 128:, :]
        elif kind == 'res4':
            @pl.when(j == j_own)
            def _(o_ref=o_ref):
                for r in range(4):
                    o_ref[:, r * WIDTH:(r + 1) * WIDTH] = acc_ref[r * 128:(r + 1) * 128, :]
        else:
            raise ValueError(kind)


def _sticky(j_own, first, last, idx_fn):
    def index_map(j, i):
        idx = idx_fn(i)
        return tuple(jnp.where(j < j_own, f, jnp.where(j > j_own, l, k))
                     for f, l, k in zip(first, last, idx))
    return index_map


def _perm_of(j):
    return jnp.where(j < 9, j % 3, 0)


def _inproj_prompt(x_bf, w_bf, b_in, cos_tab, sin_tab, ln_g, ln_b):
    n_rows = x_bf.shape[0]
    batch = n_rows // 2048
    tm = 512
    tpb = 2048 // tm
    n_i = n_rows // tm
    x4 = x_bf.reshape(batch, 512, 4 * D_MODEL)
    x16 = x_bf.reshape(batch, 128, 16 * D_MODEL)

    def x_map(perm_id, idx_fn):
        def index_map(j, i):
            idx = idx_fn(i)
            return tuple(jnp.where(_perm_of(j) == perm_id, k, 0) for k in idx)
        return index_map

    bt = lambda i: (i // tpb, 0, i % tpb)
    in_specs = [
        pl.BlockSpec((tm, D_MODEL), x_map(0, lambda i: (i, 0))),
        pl.BlockSpec((None, 512, D_MODEL), x_map(1, bt)),
        pl.BlockSpec((None, 128, 4 * D_MODEL), x_map(2, bt)),
        pl.BlockSpec((D_MODEL, WIDTH), lambda j, i: (0, j)),
        pl.BlockSpec((1, WIDTH), lambda j, i: (0, j)),
        pl.BlockSpec((None, tm, HEAD_DIM), lambda j, i: (_perm_of(j), i % tpb, 0)),
        pl.BlockSpec((None, tm, HEAD_DIM), lambda j, i: (_perm_of(j), i % tpb, 0)),
        pl.BlockSpec((1, WIDTH), lambda j, i: (0, 0)),
        pl.BlockSpec((1, WIDTH), lambda j, i: (0, 0)),
    ]
    f32_specs = []
    out_shapes = [jax.ShapeDtypeStruct((N_COL_TILES, n_rows, WIDTH), BF16)]
    out_specs = [pl.BlockSpec((None, tm, WIDTH), lambda j, i: (j, i, 0))]
    for base in (J_K0, J_V0):
        f32_specs.append((base, 'tail'))
        out_shapes.append(jax.ShapeDtypeStruct((batch, 128, WIDTH), F32))
        out_specs.append(pl.BlockSpec(
            (None, 128, WIDTH),
            _sticky(base, (0, 0, 0), (batch - 1, 0, 0), lambda i: (i // tpb, 0, 0))))
        f32_specs.append((base + 1, 'tail4'))
        out_shapes.append(jax.ShapeDtypeStruct((batch, 128, 4 * WIDTH), F32))
        out_specs.append(pl.BlockSpec(
            (None, 128, WIDTH), _sticky(base + 1, (0, 0, 0), (batch - 1, 0, tpb - 1), bt)))
        f32_specs.append((base + 2, 'res4'))
        out_shapes.append(jax.ShapeDtypeStruct((batch, 128, 16 * WIDTH), F32))
        out_specs.append(pl.BlockSpec(
            (None, 128, 4 * WIDTH), _sticky(base + 2, (0, 0, 0), (batch - 1, 0, tpb - 1), bt)))

    body = functools.partial(_inproj_body, tm=tm, permuted=True, f32_specs=tuple(f32_specs))
    outs = pl.pallas_call(
        body,
        grid=(N_COL_TILES, n_i),
        in_specs=in_specs,
        out_specs=out_specs,
        out_shape=out_shapes,
        scratch_shapes=[pltpu.VMEM((tm, WIDTH), F32)],
        compiler_params=pltpu.CompilerParams(
            dimension_semantics=("arbitrary", "arbitrary"), vmem_limit_bytes=VMEM_LIMIT),
        name="inproj_prompt",
    )(x_bf, x4, x16, w_bf, b_in, cos_tab, sin_tab, ln_g, ln_b)
    return outs


def _inproj_sample(x_bf, w_bf, b_in, cos_tab, sin_tab, ln_g, ln_b):
    n_rows = x_bf.shape[0]
    tm = 256
    n_i = n_rows // tm
    in_specs = [
        pl.BlockSpec((tm, D_MODEL), lambda j, i: (i, 0)),
        pl.BlockSpec((D_MODEL, WIDTH), lambda j, i: (0, j)),
        pl.BlockSpec((1, WIDTH), lambda j, i: (0, j)),
        pl.BlockSpec((tm, HEAD_DIM), lambda j, i: (i, 0)),
        pl.BlockSpec((tm, HEAD_DIM), lambda j, i: (i, 0)),
        pl.BlockSpec((1, WIDTH), lambda j, i: (0, 0)),
        pl.BlockSpec((1, WIDTH), lambda j, i: (0, 0)),
    ]
    own = list(range(9)) + [J_VA]
    f32_specs = tuple((jo, 'all') for jo in own)
    out_shapes = [jax.ShapeDtypeStruct((N_COL_TILES, n_rows, WIDTH), BF16)]
    out_specs = [pl.BlockSpec((None, tm, WIDTH), lambda j, i: (j, i, 0))]
    for jo in own:
        out_shapes.append(jax.ShapeDtypeStruct((n_rows, WIDTH), F32))
        out_specs.append(pl.BlockSpec(
            (tm, WIDTH), _sticky(jo, (0, 0), (n_i - 1, 0), lambda i: (i, 0))))
    body = functools.partial(_inproj_body, tm=tm, permuted=False, f32_specs=f32_specs)
    return pl.pallas_call(
        body,
        grid=(N_COL_TILES, n_i),
        in_specs=in_specs,
        out_specs=out_specs,
        out_shape=out_shapes,
        scratch_shapes=[pltpu.VMEM((tm, WIDTH), F32)],
        compiler_params=pltpu.CompilerParams(
            dimension_semantics=("arbitrary", "arbitrary"), vmem_limit_bytes=VMEM_LIMIT),
        name="inproj_sample",
    )(x_bf, w_bf, b_in, cos_tab, sin_tab, ln_g, ln_b)


def _lse_tile(cols):
    rows = cols[0].shape[0]
    lane_head = lax.broadcasted_iota(jnp.int32, (rows, HEADS * LSE_REP), 1) >> LSE_REP_LOG2
    tile = jnp.zeros((rows, HEADS * LSE_REP), F32)
    for h, c in enumerate(cols):
        tile = jnp.where(lane_head == h, c, tile)
    return tile


def _attn_prompt_body(*refs, n_blocks, has_prev, emit_lse):
    q_ref, k_ref, v_ref = refs[:3]
    refs = refs[3:]
    if has_prev:
        oprev_ref, lprev_ref = refs[:2]
        refs = refs[2:]
    o_ref = refs[0]
    lse_ref = refs[1] if emit_lse else None
    scale = np.float32(HEAD_DIM ** -0.5)
    win = 128 if n_blocks == 1 else 256

    def block(n, carry):
        q0 = pl.multiple_of(n * 128, 128)
        if n_blocks == 1:
            k0 = 0
            off = 0
        else:
            kb = jnp.maximum(n - 1, 0)
            k0 = pl.multiple_of(kb * 128, 128)
            off = (n - kb) * 128
        qi = lax.broadcasted_iota(jnp.int32, (128, win), 0)
        ki = lax.broadcasted_iota(jnp.int32, (128, win), 1)
        diff = qi - ki + off
        mask = (diff >= 0) & (diff <= 128)
        lse_cols = []
        if has_prev:
            lprev = lprev_ref[pl.ds(q0, 128), :]
        for h in range(HEADS):
            lanes = slice(h * HEAD_DIM, (h + 1) * HEAD_DIM)
            q = q_ref[pl.ds(q0, 128), lanes]
            k = k_ref[pl.ds(k0, win), lanes]
            v = v_ref[pl.ds(k0, win), lanes]
            s = lax.dot_general(q, k, (((1,), (1,)), ((), ())),
                                preferred_element_type=F32) * scale
            s = jnp.where(mask, s, -jnp.inf)
            m = jnp.max(s, axis=-1, keepdims=True)
            p = jnp.exp(s - m)
            den = jnp.sum(p, axis=-1, keepdims=True)
            o = jnp.dot(p.astype(BF16), v, preferred_element_type=F32) / den
            lse = m + jnp.log(den)
            if has_prev:
                lp = lprev[:, h * LSE_REP:h * LSE_REP + 1]
                op = oprev_ref[pl.ds(q0, 128), lanes].astype(F32)
                mx = jnp.maximum(lp, lse)
                wp = jnp.exp(lp - mx)
                wn = jnp.exp(lse - mx)
                tot = wp + wn
                o = (wp * op + wn * o) / tot
                lse = mx + jnp.log(tot)
            o_ref[pl.ds(q0, 128), lanes] = o.astype(o_ref.dtype)
            lse_cols.append(lse)
        if emit_lse:
            lse_ref[pl.ds(q0, 128), :] = _lse_tile(lse_cols)
        return carry

    if n_blocks == 1:
        block(0, 0)
    else:
        lax.fori_loop(0, n_blocks, block, 0)


def _attn_prompt_group(h3d, g, batch, seq, o_prev, lse_prev, emit_lse):
    _, dil = DIL_GROUPS[g]
    sub_len = seq // dil
    n_blocks = sub_len // 128
    n_rows = h3d.shape[1]
    hv = h3d.reshape(N_COL_TILES, n_rows // sub_len, sub_len, WIDTH)
    has_prev = o_prev is not None
    tile = (None, None, sub_len, WIDTH)
    in_specs = [
        pl.BlockSpec(tile, lambda b, r, g=g: (g, b * dil + r, 0, 0)),
        pl.BlockSpec(tile, lambda b, r, g=g: (J_K0 + g, b * dil + r, 0, 0)),
        pl.BlockSpec(tile, lambda b, r, g=g: (J_V0 + g, b * dil + r, 0, 0)),
    ]
    args = [hv, hv, hv]
    nat_o = pl.BlockSpec((None, sub_len, WIDTH), lambda b, r: (b, 0, r))
    nat_l = pl.BlockSpec((None, sub_len, HEADS * LSE_REP), lambda b, r: (b, 0, r))
    if has_prev:
        in_specs += [nat_o, nat_l]
        args += [o_prev.reshape(batch, sub_len, dil * WIDTH),
                 lse_prev.reshape(batch, sub_len, dil * HEADS * LSE_REP)]
    out_shapes = [jax.ShapeDtypeStruct((batch, sub_len, dil * WIDTH), BF16)]
    out_specs = [nat_o]
    if emit_lse:
        out_shapes.append(jax.ShapeDtypeStruct((batch, sub_len, dil * HEADS * LSE_REP), F32))
        out_specs.append(nat_l)
    body = functools.partial(_attn_prompt_body, n_blocks=n_blocks,
                             has_prev=has_prev, emit_lse=emit_lse)
    outs = pl.pallas_call(
        body,
        grid=(batch, dil),
        in_specs=in_specs,
        out_specs=out_specs,
        out_shape=out_shapes,
        compiler_params=pltpu.CompilerParams(
            dimension_semantics=("arbitrary", "arbitrary"), vmem_limit_bytes=VMEM_LIMIT),
        name="attn_prompt_g%d" % g,
    )(*args)
    o = outs[0].reshape(batch, seq, WIDTH)
    lse = outs[1].reshape(batch, seq, HEADS * LSE_REP) if emit_lse else None
    return o, lse


def _attn_sample_body(q0_ref, q1_ref, q2_ref, kn0_ref, kn1_ref, kn2_ref,
                      vn0_ref, vn1_ref, vn2_ref,
                      ck0_ref, cv0_ref, ck1_ref, cv1_ref, ck2_ref, cv2_ref, o_ref, *, n_new):
    scale = np.float32(HEAD_DIM ** -0.5)
    q_refs = (q0_ref, q1_ref, q2_ref)
    kn_refs = (kn0_ref, kn1_ref, kn2_ref)
    vn_refs = (vn0_ref, vn1_ref, vn2_ref)
    ck_refs = (ck0_ref, ck1_ref, ck2_ref)
    cv_refs = (cv0_ref, cv1_ref, cv2_ref)
    t_new = lax.broadcasted_iota(jnp.int32, (n_new, n_new), 0)
    n_idx = lax.broadcasted_iota(jnp.int32, (n_new, n_new), 1)
    t_row = lax.broadcasted_iota(jnp.int32, (n_new, 128), 0)
    l_col = lax.broadcasted_iota(jnp.int32, (n_new, 128), 1)

    for h in range(HEADS):
        lanes = slice(h * HEAD_DIM, (h + 1) * HEAD_DIM)
        outs, lses = [], []
        for g, (_, dil) in enumerate(DIL_GROUPS):
            q = q_refs[g][:, lanes].astype(BF16)
            kn = kn_refs[g][:, lanes].astype(BF16)
            vn = vn_refs[g][:, lanes].astype(BF16)
            n_res = min(dil, n_new)
            s_new = lax.dot_general(q, kn, (((1,), (1,)), ((), ())),
                                    preferred_element_type=F32) * scale
            dil_log2 = dil.bit_length() - 1
            dn = t_new - n_idx
            new_ok = (dn >= 0) & ((dn & (dil - 1)) == 0)
            s_new = jnp.where(new_ok, s_new, -jnp.inf)
            s_res, ok_res = [], []
            for r in range(n_res):
                cl = slice(r * WIDTH + h * HEAD_DIM, r * WIDTH + (h + 1) * HEAD_DIM)
                kc = ck_refs[g][:, cl].astype(BF16)
                s = lax.dot_general(q, kc, (((1,), (1,)), ((), ())),
                                    preferred_element_type=F32) * scale
                ok = ((t_row & (dil - 1)) == r) & (l_col >= (t_row >> dil_log2))
                s_res.append(jnp.where(ok, s, -jnp.inf))
                ok_res.append(ok)
            m = jnp.max(s_new, axis=-1, keepdims=True)
            for s in s_res:
                m = jnp.maximum(m, jnp.max(s, axis=-1, keepdims=True))
            p_new = jnp.exp(s_new - m)
            den = jnp.sum(p_new, axis=-1, keepdims=True)
            acc = jnp.dot(p_new.astype(BF16), vn, preferred_element_type=F32)
            for r in range(n_res):
                cl = slice(r * WIDTH + h * HEAD_DIM, r * WIDTH + (h + 1) * HEAD_DIM)
                p = jnp.exp(s_res[r] - m)
                den = den + jnp.sum(p, axis=-1, keepdims=True)
                vc = cv_refs[g][:, cl].astype(BF16)
                acc = acc + jnp.dot(p.astype(BF16), vc, preferred_element_type=F32)
            outs.append(acc / den)
            lses.append(m + jnp.log(den))
        mx = jnp.maximum(jnp.maximum(lses[0], lses[1]), lses[2])
        ws = [jnp.exp(l - mx) for l in lses]
        tot = ws[0] + ws[1] + ws[2]
        o = (ws[0] * outs[0] + ws[1] * outs[1] + ws[2] * outs[2]) / tot
        o_ref[:, lanes] = o


def _attn_sample(q_f32, kn_f32, vn_f32, caches_k, caches_v, n_batch, n_new):
    row_spec = pl.BlockSpec((n_new, WIDTH), lambda b: (b, 0))
    in_specs = [row_spec] * 9
    args = list(q_f32) + list(kn_f32) + list(vn_f32)
    for g, (win, dil) in enumerate(DIL_GROUPS):
        n_res = min(dil, n_new)
        for c in (caches_k[g], caches_v[g]):
            in_specs.append(pl.BlockSpec((None, win // dil, n_res * WIDTH), lambda b: (b, 0, 0)))
            args.append(c.reshape(n_batch, win // dil, dil * WIDTH))
    body = functools.partial(_attn_sample_body, n_new=n_new)
    return pl.pallas_call(
        body,
        grid=(n_batch,),
        in_specs=in_specs,
        out_specs=pl.BlockSpec((n_new, WIDTH), lambda b: (b, 0)),
        out_shape=jax.ShapeDtypeStruct((n_batch * n_new, WIDTH), F32),
        compiler_params=pltpu.CompilerParams(
            dimension_semantics=("arbitrary",), vmem_limit_bytes=VMEM_LIMIT),
        name="attn_sample",
    )(*args)


def _final_body(zb_ref, gu_ref, vn_ref, za_ref, ga0_ref, ga1_ref, gb0_ref, gb1_ref,
                o_ref, x_ref, wmix_ref, bmix_ref, woa_ref, wob_ref, wout_ref,
                lng_ref, lnb_ref, y_ref, ya_ref, *, tm, mix_rows, alpha):
    row = lax.broadcasted_iota(jnp.int32, (CHUNK, CHUNK), 0)
    col = lax.broadcasted_iota(jnp.int32, (CHUNK, CHUNK), 1)
    mix_log2 = mix_rows.bit_length() - 1
    causal = (row >= col) & ((row >> mix_log2) == (col >> mix_log2))
    for g in range(HEADS):
        lanes = slice(g * CHUNK, (g + 1) * CHUNK)
        w_c = jnp.where(causal, wmix_ref[g], 0.0).astype(BF16)
        for c in range(tm // CHUNK):
            rows = slice(c * CHUNK, (c + 1) * CHUNK)
            mixed = jnp.dot(w_c, vn_ref[rows, lanes], preferred_element_type=F32)
            mixed = mixed + bmix_ref[:, lanes]
            ya = gu_ref[rows, lanes].astype(F32) * mixed * za_ref[rows, lanes].astype(F32)
            ya_ref[rows, lanes] = ya.astype(BF16)
    proj_a = jnp.dot(ya_ref[...], woa_ref[...], preferred_element_type=F32)
    yb = (o_ref[...].astype(F32) * zb_ref[...].astype(F32)).astype(BF16)
    proj_b = jnp.dot(yb, wob_ref[...], preferred_element_type=F32)
    ga = jnp.concatenate([ga0_ref[...], ga1_ref[...]], axis=1).astype(F32)
    gb = jnp.concatenate([gb0_ref[...], gb1_ref[...]], axis=1).astype(F32)
    merged = (ga * proj_a + gb * proj_b).astype(BF16)
    z = alpha * x_ref[...] + jnp.dot(merged, wout_ref[...], preferred_element_type=F32)
    y_ref[...] = _layer_norm_rows(z, lng_ref[...], lnb_ref[...])


def _final(h3d, o, x, wmix, bmix, woa, wob, wout, ln_g, ln_b, mix_rows, alpha, name):
    n_rows = x.shape[0]
    tm = 256
    hspec = lambda jj: pl.BlockSpec((None, tm, WIDTH), lambda i, jj=jj: (jj, i, 0))
    const = lambda shape: pl.BlockSpec(shape, lambda i: (0,) * len(shape),
                                       pipeline_mode=pl.Buffered(1))
    in_specs = [hspec(J_ZB), hspec(J_U), hspec(J_VA), hspec(J_ZA),
                hspec(J_GA), hspec(J_GA + 1), hspec(J_GB), hspec(J_GB + 1),
                pl.BlockSpec((tm, WIDTH), lambda i: (i, 0)),
                pl.BlockSpec((tm, D_MODEL), lambda i: (i, 0)),
                const((HEADS, CHUNK, CHUNK)), const((CHUNK, WIDTH)),
                const((WIDTH, D_MODEL)), const((WIDTH, D_MODEL)), const((D_MODEL, D_MODEL)),
                const((1, D_MODEL)), const((1, D_MODEL))]
    body = functools.partial(_final_body, tm=tm, mix_rows=mix_rows, alpha=np.float32(alpha))
    return pl.pallas_call(
        body,
        grid=(n_rows // tm,),
        in_specs=in_specs,
        out_specs=pl.BlockSpec((tm, D_MODEL), lambda i: (i, 0)),
        out_shape=jax.ShapeDtypeStruct((n_rows, D_MODEL), F32),
        scratch_shapes=[pltpu.VMEM((tm, WIDTH), BF16)],
        compiler_params=pltpu.CompilerParams(
            dimension_semantics=("arbitrary",), vmem_limit_bytes=VMEM_LIMIT),
        name=name,
    )(h3d, h3d, h3d, h3d, h3d, h3d, h3d, h3d, o, x, wmix, bmix, woa, wob, wout, ln_g, ln_b)


def _rope_tables(pos):
    half = HEAD_DIM // 2
    inv = ROPE_THETA ** (-jnp.arange(0, half, dtype=F32) * 2.0 / HEAD_DIM)
    ang = pos.astype(F32)[:, None] * inv[None, :]
    cos = jnp.cos(ang)
    sin = jnp.sin(ang)
    return jnp.concatenate([cos, cos], axis=-1), jnp.concatenate([-sin, sin], axis=-1)


def _layer(xp, xs, caches_k, caches_v, w_in, b_in, w_s, b_s, ln_v_g, ln_v_b,
           w_o_a, w_o_b, w_out, ln_g, ln_b, past_len, alpha):
    batch, seq, _ = xp.shape
    n_dec, n_new, _ = xs.shape
    assert seq == 2048 and n_new == 8 and xp.shape[2] == D_MODEL
    for g, (win, _) in enumerate(DIL_GROUPS):
        assert caches_k[g].shape == (n_dec, win, HEADS, HEAD_DIM)

    w_bf = w_in.astype(BF16)
    b2 = b_in.reshape(1, -1)
    lvg = ln_v_g.reshape(1, WIDTH)
    lvb = ln_v_b.reshape(1, WIDTH)
    lg = ln_g.reshape(1, D_MODEL)
    lb = ln_b.reshape(1, D_MODEL)

    pos = jnp.arange(seq, dtype=jnp.int32)
    pos_tabs = [pos]
    for _, dil in DIL_GROUPS[1:]:
        pos_tabs.append(pos.reshape(seq // dil, dil).T.reshape(seq))
    cos_p, sin_p = _rope_tables(jnp.stack(pos_tabs).reshape(-1))
    cos_p = cos_p.reshape(N_DIL, seq, HEAD_DIM)
    sin_p = sin_p.reshape(N_DIL, seq, HEAD_DIM)
    pos_s = past_len + jnp.tile(jnp.arange(n_new, dtype=jnp.int32), n_dec)
    cos_s, sin_s = _rope_tables(pos_s)

    xp2 = xp.reshape(batch * seq, D_MODEL)
    xs2 = xs.reshape(n_dec * n_new, D_MODEL)

    (h3d, k0, k1, k2, v0, v1, v2) = _inproj_prompt(
        xp2.astype(BF16), w_bf, b2, cos_p, sin_p, lvg, lvb)
    o, lse = None, None
    for g in range(N_DIL):
        o, lse = _attn_prompt_group(h3d, g, batch, seq, o, lse, emit_lse=(g < N_DIL - 1))

    wmix_p = w_s
    bmix_p = jnp.repeat(b_s.T, CHUNK, axis=1)
    woa = w_o_a.astype(BF16)
    wob = w_o_b.astype(BF16)
    wout = w_out.astype(BF16)
    yp = _final(h3d, o.reshape(batch * seq, WIDTH), xp2, wmix_p, bmix_p, woa, wob, wout,
                lg, lb, CHUNK, alpha, "final_prompt")

    souts = _inproj_sample(xs2.astype(BF16), w_bf, b2, cos_s, sin_s, lvg, lvb)
    hs3d = souts[0]
    q_s, kn_s, vn_s, gv_s = souts[1:4], souts[4:7], souts[7:10], souts[10]
    ck = [c.reshape(n_dec, c.shape[1], WIDTH) for c in caches_k]
    cv = [c.reshape(n_dec, c.shape[1], WIDTH) for c in caches_v]
    o_s = _attn_sample(q_s, kn_s, vn_s, ck, cv, n_dec, n_new)
    reps = CHUNK // n_new
    wmix_s = jnp.tile(w_s[:, :n_new, :n_new], (1, reps, reps))
    bmix_s = jnp.repeat(jnp.tile(b_s[:, :n_new], (1, reps)).T, CHUNK, axis=1)
    ys = _final(hs3d, o_s, xs2, wmix_s, bmix_s, woa, wob, wout, lg, lb, n_new, alpha,
                "final_sample")

    shp_p = lambda a, keep: a.reshape(batch, keep, HEADS, HEAD_DIM)
    shp_s = lambda a: a.reshape(n_dec, n_new, HEADS, HEAD_DIM)
    new_p = (shp_p(k0, 128), shp_p(v0, 128), shp_p(k1, 512), shp_p(v1, 512),
             shp_p(k2, 2048), shp_p(v2, 2048))
    new_s = tuple(shp_s(a) for pair in zip(kn_s, vn_s) for a in pair)
    return (yp.reshape(batch, seq, D_MODEL), ys.reshape(n_dec, n_new, D_MODEL),
            new_p, new_s, gv_s.reshape(n_dec, n_new, WIDTH))


def kernel(x_prompt, x_sample, cache_k_w128, cache_v_w128, cache_k_w512, cache_v_w512,
           cache_k_w2048, cache_v_w2048, w_in, b_in, w_s, b_s, ln_v_g, ln_v_b,
           w_o_a, w_o_b, w_out, ln_g, ln_b):
    depth = w_in.shape[0]
    assert depth == 1, "single-layer step"
    past_len = cache_k_w2048.shape[2]
    alpha = float(2 * depth) ** 0.25
    l = 0
    caches_k = (cache_k_w128[l], cache_k_w512[l], cache_k_w2048[l])
    caches_v = (cache_v_w128[l], cache_v_w512[l], cache_v_w2048[l])
    yp, ys, new_p, new_s, gv = _layer(
        x_prompt, x_sample, caches_k, caches_v, w_in[l], b_in[l], w_s[l], b_s[l],
        ln_v_g[l], ln_v_b[l], w_o_a[l], w_o_b[l], w_out[l], ln_g[l], ln_b[l],
        past_len, alpha)
    return (yp, ys) + tuple(a[None] for a in new_p) + tuple(a[None] for a in new_s) + (gv[None],)
```

```python
import functools

import numpy as np
import jax
import jax.numpy as jnp
from jax import lax
from jax.experimental import pallas as pl
from jax.experimental.pallas import tpu as pltpu

F32 = jnp.float32
BF16 = jnp.bfloat16

D_MODEL = 2048
HEAD_DIM = 128
HEADS = 8
DIL_GROUPS = ((128, 1), (512, 4), (2048, 16))
N_DIL = len(DIL_GROUPS)
BAND = 128
WIDTH = HEADS * HEAD_DIM
CHUNK = 128
SEQ = 2048
ROPE_THETA = 10000.0
LN_EPS = 1e-5
N_COL_TILES = 17
J_K0, J_V0, J_ZB, J_U, J_VA, J_ZA, J_GA, J_GB = 3, 6, 9, 10, 11, 12, 13, 15
EPILOGUE_ROWS = 64
STAGE_CHUNK = 256
STAGE_HEADS = 4
STAGE_RESIDUES = 2
LSE_REP_LOG2 = 4
LSE_REP = 1 << LSE_REP_LOG2
VMEM_LIMIT = 56 * 1024 * 1024


def _gelu(x):
    return 0.5 * x * (1.0 + lax.erf(x * np.float32(np.sqrt(0.5))))


def _sigmoid(x):
    return 1.0 / (1.0 + jnp.exp(-x))


def _silu(x):
    return x * _sigmoid(x)


def _layer_norm_rows(x, g, b):
    mu = jnp.mean(x, axis=-1, keepdims=True)
    xc = x - mu
    var = jnp.mean(xc * xc, axis=-1, keepdims=True)
    return xc * lax.rsqrt(var + LN_EPS) * g + b


def _inproj_body(x_ref, w_ref, b_ref, cos_ref, sin_ref, lng_ref, lnb_ref, h_ref, *rest,
                 tm, f32_specs):
    f32_refs = rest[:len(f32_specs)]
    acc_ref = rest[-1]
    j = pl.program_id(0)
    i = pl.program_id(1)
    tiles_per_batch = SEQ // tm
    last_of_batch = i % tiles_per_batch == tiles_per_batch - 1

    acc_ref[...] = jnp.dot(x_ref[...], w_ref[...], preferred_element_type=F32)

    def run_epilogue(fn):
        def step(c, carry):
            r0 = pl.multiple_of(c * EPILOGUE_ROWS, EPILOGUE_ROWS)
            a = acc_ref[pl.ds(r0, EPILOGUE_ROWS), :] + b_ref[...]
            res = fn(a, r0)
            acc_ref[pl.ds(r0, EPILOGUE_ROWS), :] = res
            h_ref[pl.ds(r0, EPILOGUE_ROWS), :] = res.astype(BF16)
            return carry
        lax.fori_loop(0, tm // EPILOGUE_ROWS, step, 0)

    def rope(a, r0):
        cos = cos_ref[pl.ds(r0, EPILOGUE_ROWS), :]
        sin = sin_ref[pl.ds(r0, EPILOGUE_ROWS), :]
        parts = []
        for h in range(HEADS):
            ah = a[:, h * HEAD_DIM:(h + 1) * HEAD_DIM]
            parts.append(ah * cos + pltpu.roll(ah, HEAD_DIM // 2, 1) * sin)
        return jnp.concatenate(parts, axis=1)

    @pl.when(j < J_V0)
    def _():
        run_epilogue(rope)

    @pl.when((j >= J_V0) & (j < J_ZB))
    def _():
        run_epilogue(lambda a, r0: a)

    @pl.when((j == J_ZB) | (j == J_ZA))
    def _():
        run_epilogue(lambda a, r0: _silu(a))

    @pl.when(j == J_U)
    def _():
        run_epilogue(lambda a, r0: _gelu(a))

    @pl.when(j == J_VA)
    def _():
        run_epilogue(lambda a, r0: _layer_norm_rows(_gelu(a), lng_ref[...], lnb_ref[...]))

    @pl.when(j >= J_GA)
    def _():
        run_epilogue(lambda a, r0: _sigmoid(a))

    def scatter_heads(o_ref, row0, n_rows):
        for h in range(HEADS):
            o_ref[:, h, :] = acc_ref[row0:row0 + n_rows, h * HEAD_DIM:(h + 1) * HEAD_DIM]

    for (j_own, kind), o_ref in zip(f32_specs, f32_refs):
        if kind == 'heads':
            @pl.when(j == j_own)
            def _(o_ref=o_ref):
                scatter_heads(o_ref, 0, tm)
        elif kind == 'heads_last':
            @pl.when((j == j_own) & last_of_batch)
            def _(o_ref=o_ref):
                scatter_heads(o_ref, 0, tm)
        elif kind == 'heads_tail':
            @pl.when((j == j_own) & last_of_batch)
            def _(o_ref=o_ref):
                scatter_heads(o_ref, tm - 128, 128)
        elif kind == 'flat':
            @pl.when(j == j_own)
            def _(o_ref=o_ref):
                o_ref[...] = acc_ref[...]
        else:
            raise ValueError(kind)


def _sticky(j_own, first, last, idx_fn):
    def index_map(j, i):
        idx = idx_fn(i)
        return tuple(jnp.where(j < j_own, f, jnp.where(j > j_own, l, k))
                     for f, l, k in zip(first, last, idx))
    return index_map


def _inproj_call(x_bf, w_bf, b_in, cos_tab, sin_tab, ln_g, ln_b, *, tm, table_tiles,
                 f32_specs, f32_shapes, f32_blocks, name):
    n_rows = x_bf.shape[0]
    n_i = n_rows // tm
    in_specs = [
        pl.BlockSpec((tm, D_MODEL), lambda j, i: (i, 0)),
        pl.BlockSpec((D_MODEL, WIDTH), lambda j, i: (0, j)),
        pl.BlockSpec((1, WIDTH), lambda j, i: (0, j)),
        pl.BlockSpec((tm, HEAD_DIM), lambda j, i: (i % table_tiles, 0)),
        pl.BlockSpec((tm, HEAD_DIM), lambda j, i: (i % table_tiles, 0)),
        pl.BlockSpec((1, WIDTH), lambda j, i: (0, 0)),
        pl.BlockSpec((1, WIDTH), lambda j, i: (0, 0)),
    ]
    out_shapes = [jax.ShapeDtypeStruct((N_COL_TILES, n_rows, WIDTH), BF16)] + list(f32_shapes)
    out_specs = [pl.BlockSpec((None, tm, WIDTH), lambda j, i: (j, i, 0))] + list(f32_blocks)
    body = functools.partial(_inproj_body, tm=tm, f32_specs=tuple(f32_specs))
    return pl.pallas_call(
        body,
        grid=(N_COL_TILES, n_i),
        in_specs=in_specs,
        out_specs=out_specs,
        out_shape=out_shapes,
        scratch_shapes=[pltpu.VMEM((tm, WIDTH), F32)],
        compiler_params=pltpu.CompilerParams(
            dimension_semantics=("arbitrary", "arbitrary"), vmem_limit_bytes=VMEM_LIMIT),
        name=name,
    )(x_bf, w_bf, b_in, cos_tab, sin_tab, ln_g, ln_b)


def _inproj_prompt(x_bf, w_bf, b_in, cos_tab, sin_tab, ln_g, ln_b):
    batch = x_bf.shape[0] // SEQ
    tm = 512
    tpb = SEQ // tm
    specs, shapes, blocks = [], [], []
    for base in (J_K0, J_V0):
        for g, (win, _) in enumerate(DIL_GROUPS):
            keep = min(win, SEQ)
            shapes.append(jax.ShapeDtypeStruct((batch, keep, HEADS, HEAD_DIM), F32))
            last = (batch - 1, keep // tm - 1 if keep >= tm else 0, 0, 0)
            if keep < tm:
                specs.append((base + g, 'heads_tail'))
                idx_fn = lambda i: (i // tpb, 0, 0, 0)
                rows = keep
            elif keep == tm:
                specs.append((base + g, 'heads_last'))
                idx_fn = lambda i: (i // tpb, 0, 0, 0)
                rows = tm
            else:
                assert keep == SEQ
                specs.append((base + g, 'heads'))
                idx_fn = lambda i: (i // tpb, i % tpb, 0, 0)
                rows = tm
            blocks.append(pl.BlockSpec((None, rows, HEADS, HEAD_DIM),
                                       _sticky(base + g, (0, 0, 0, 0), last, idx_fn)))
    return _inproj_call(x_bf, w_bf, b_in, cos_tab, sin_tab, ln_g, ln_b, tm=tm,
                        table_tiles=tpb, f32_specs=specs, f32_shapes=shapes,
                        f32_blocks=blocks, name="inproj_prompt")


def _inproj_sample(x_bf, w_bf, b_in, cos_tab, sin_tab, ln_g, ln_b):
    n_rows = x_bf.shape[0]
    tm = 256
    n_i = n_rows // tm
    specs, shapes, blocks = [], [], []
    for jo in range(J_ZB):
        specs.append((jo, 'heads'))
        shapes.append(jax.ShapeDtypeStruct((n_rows, HEADS, HEAD_DIM), F32))
        blocks.append(pl.BlockSpec((tm, HEADS, HEAD_DIM),
                                   _sticky(jo, (0, 0, 0), (n_i - 1, 0, 0), lambda i: (i, 0, 0))))
    specs.append((J_VA, 'flat'))
    shapes.append(jax.ShapeDtypeStruct((n_rows, WIDTH), F32))
    blocks.append(pl.BlockSpec((tm, WIDTH),
                               _sticky(J_VA, (0, 0), (n_i - 1, 0), lambda i: (i, 0))))
    return _inproj_call(x_bf, w_bf, b_in, cos_tab, sin_tab, ln_g, ln_b, tm=tm,
                        table_tiles=n_i, f32_specs=specs, f32_shapes=shapes,
                        f32_blocks=blocks, name="inproj_sample")


def _lse_tile(cols):
    rows = cols[0].shape[0]
    lane_head = lax.broadcasted_iota(jnp.int32, (rows, HEADS * LSE_REP), 1) >> LSE_REP_LOG2
    tile = jnp.zeros((rows, HEADS * LSE_REP), F32)
    for h, c in enumerate(cols):
        tile = jnp.where(lane_head == h, c, tile)
    return tile


def _band_softmax(qkvs, off):
    win = qkvs[0][1].shape[0]
    scale = np.float32(HEAD_DIM ** -0.5)
    qi = lax.broadcasted_iota(jnp.int32, (BAND, win), 0)
    ki = lax.broadcasted_iota(jnp.int32, (BAND, win), 1)
    diff = qi - ki + off
    mask = (diff >= 0) & (diff <= BAND)
    scores = [lax.dot_general(q, k, (((1,), (1,)), ((), ())), preferred_element_type=F32)
              for q, k, _ in qkvs]
    probs, dens, lses = [], [], []
    for s in scores:
        s = jnp.where(mask, s * scale, -jnp.inf)
        m = jnp.max(s, axis=-1, keepdims=True)
        p = jnp.exp(s - m)
        den = jnp.sum(p, axis=-1, keepdims=True)
        probs.append(p.astype(BF16))
        dens.append(den)
        lses.append(m + jnp.log(den))
    outs = [jnp.dot(p, v, preferred_element_type=F32) for p, (_, _, v) in zip(probs, qkvs)]
    return [(o / den, lse) for o, den, lse in zip(outs, dens, lses)]


def _attn_prompt_body(q_ref, k_ref, v_ref, o_ref, lse_ref, *scratch, dil):
    sub_len = SEQ // dil
    n_blocks = sub_len // BAND
    nb_log2 = n_blocks.bit_length() - 1

    if dil == 1:
        def block(n, carry):
            q0 = pl.multiple_of(n * BAND, BAND)
            kb = jnp.maximum(n - 1, 0)
            k0 = pl.multiple_of(kb * BAND, BAND)
            off = (n - kb) * BAND
            head_lanes = [slice(h * HEAD_DIM, (h + 1) * HEAD_DIM) for h in range(HEADS)]
            res = _band_softmax([(q_ref[pl.ds(q0, BAND), lanes],
                                  k_ref[pl.ds(k0, 2 * BAND), lanes],
                                  v_ref[pl.ds(k0, 2 * BAND), lanes]) for lanes in head_lanes],
                                off)
            for lanes, (o, _) in zip(head_lanes, res):
                o_ref[pl.ds(q0, BAND), lanes] = o.astype(BF16)
            lse_ref[pl.ds(q0, BAND), :] = _lse_tile([lse for _, lse in res])
            return carry
        lax.fori_loop(0, n_blocks, block, 0)
        return

    qs_ref, ks_ref, vs_ref, os_ref, ls_ref = scratch
    m = STAGE_CHUNK // dil
    m_log2 = m.bit_length() - 1
    n_chunks = SEQ // STAGE_CHUNK
    row = lax.broadcasted_iota(jnp.int32, (STAGE_CHUNK, STAGE_CHUNK), 0)
    col = lax.broadcasted_iota(jnp.int32, (STAGE_CHUNK, STAGE_CHUNK), 1)
    to_staged = col == ((row & (m - 1)) * dil + (row >> m_log2))
    to_natural = row == ((col & (m - 1)) * dil + (col >> m_log2))
    perm = jnp.where(to_staged, 1.0, 0.0).astype(BF16)
    unperm = jnp.where(to_natural, 1.0, 0.0).astype(BF16)
    unperm_f32 = jnp.where(to_natural, 1.0, 0.0).astype(F32)

    def staged_rows(c, r):
        return pl.ds(pl.multiple_of(r * sub_len + c * m, m), m)

    def gather_chunk(ref, c):
        return jnp.concatenate([ref[staged_rows(c, r), :] for r in range(dil)], axis=0)

    def natural_rows(c):
        return pl.ds(pl.multiple_of(c * STAGE_CHUNK, STAGE_CHUNK), STAGE_CHUNK)

    for hp in range(HEADS // STAGE_HEADS):
        pass_lanes = slice(hp * STAGE_HEADS * HEAD_DIM, (hp + 1) * STAGE_HEADS * HEAD_DIM)

        def stage(c, carry, pass_lanes=pass_lanes):
            for src, dst in ((q_ref, qs_ref), (k_ref, ks_ref), (v_ref, vs_ref)):
                y = jnp.dot(perm, src[natural_rows(c), pass_lanes],
                            preferred_element_type=F32).astype(BF16)
                for r in range(dil):
                    dst[staged_rows(c, r), :] = y[r * m:(r + 1) * m, :]
            return carry
        lax.fori_loop(0, n_chunks, stage, 0)

        def tile(t, carry, hp=hp):
            n = t & (n_blocks - 1)
            r0 = (t >> nb_log2) * STAGE_RESIDUES
            if n_blocks == 1:
                kb = 0
                off = 0
            else:
                kb = jnp.maximum(n - 1, 0)
                off = (n - kb) * BAND
            key_rows = min(2 * BAND, sub_len)
            head_lanes = [slice(hh * HEAD_DIM, (hh + 1) * HEAD_DIM) for hh in range(STAGE_HEADS)]
            items, dests = [], []
            for rr in range(STAGE_RESIDUES):
                base = (r0 + rr) * sub_len
                q_rows = pl.ds(pl.multiple_of(base + n * BAND, BAND), BAND)
                k_rows = pl.ds(pl.multiple_of(base + kb * BAND, BAND), key_rows)
                for hh, lanes in enumerate(head_lanes):
                    items.append((qs_ref[q_rows, lanes], ks_ref[k_rows, lanes],
                                  vs_ref[k_rows, lanes]))
                    dests.append((q_rows, lanes, hp * STAGE_HEADS + hh))
            for (q_rows, lanes, h), (o, lse) in zip(dests, _band_softmax(items, off)):
                os_ref[q_rows, lanes] = o.astype(BF16)
                ls_ref[q_rows, h * LSE_REP:(h + 1) * LSE_REP] = jnp.broadcast_to(
                    lse, (BAND, LSE_REP))
            return carry
        lax.fori_loop(0, dil * n_blocks // STAGE_RESIDUES, tile, 0)

        def unstage(c, carry, pass_lanes=pass_lanes):
            o_ref[natural_rows(c), pass_lanes] = jnp.dot(
                unperm, gather_chunk(os_ref, c), preferred_element_type=F32).astype(BF16)
            return carry
        lax.fori_loop(0, n_chunks, unstage, 0)

    def unstage_lse(c, carry):
        lse_ref[natural_rows(c), :] = jnp.dot(unperm_f32, gather_chunk(ls_ref, c),
                                              precision=lax.Precision.HIGHEST,
                                              preferred_element_type=F32)
        return carry
    lax.fori_loop(0, n_chunks, unstage_lse, 0)


def _attn_prompt_group(h3d, g, batch):
    _, dil = DIL_GROUPS[g]
    tile = (None, SEQ, WIDTH)
    in_specs = [
        pl.BlockSpec(tile, lambda b, g=g: (g, b, 0)),
        pl.BlockSpec(tile, lambda b, g=g: (J_K0 + g, b, 0)),
        pl.BlockSpec(tile, lambda b, g=g: (J_V0 + g, b, 0)),
    ]
    scratch = [] if dil == 1 else (
        [pltpu.VMEM((SEQ, STAGE_HEADS * HEAD_DIM), BF16)] * 4
        + [pltpu.VMEM((SEQ, HEADS * LSE_REP), F32)])
    return pl.pallas_call(
        functools.partial(_attn_prompt_body, dil=dil),
        grid=(batch,),
        in_specs=in_specs,
        out_specs=[pl.BlockSpec((SEQ, WIDTH), lambda b: (b, 0)),
                   pl.BlockSpec((SEQ, HEADS * LSE_REP), lambda b: (b, 0))],
        out_shape=[jax.ShapeDtypeStruct((batch * SEQ, WIDTH), BF16),
                   jax.ShapeDtypeStruct((batch * SEQ, HEADS * LSE_REP), F32)],
        scratch_shapes=scratch,
        compiler_params=pltpu.CompilerParams(
            dimension_semantics=("arbitrary",), vmem_limit_bytes=VMEM_LIMIT),
        name="attn_prompt_g%d" % g,
    )(h3d, h3d, h3d)


def _attn_sample_body(q0_ref, q1_ref, q2_ref, kn0_ref, kn1_ref, kn2_ref,
                      vn0_ref, vn1_ref, vn2_ref,
                      ck0_ref, cv0_ref, ck1_ref, cv1_ref, ck2_ref, cv2_ref, o_ref,
                      bc0_ref, bc1_ref, bc2_ref, bn_ref, *, n_new):
    scale = np.float32(HEAD_DIM ** -0.5)
    q_refs = (q0_ref, q1_ref, q2_ref)
    kn_refs = (kn0_ref, kn1_ref, kn2_ref)
    vn_refs = (vn0_ref, vn1_ref, vn2_ref)
    ck_refs = (ck0_ref, ck1_ref, ck2_ref)
    cv_refs = (cv0_ref, cv1_ref, cv2_ref)
    bc_refs = (bc0_ref, bc1_ref, bc2_ref)
    n_q = n_new * HEADS
    heads_log2 = HEADS.bit_length() - 1

    @pl.when(pl.program_id(0) == 0)
    def _():
        for g, (_, dil) in enumerate(DIL_GROUPS):
            n_res = min(dil, n_new)
            dil_log2 = dil.bit_length() - 1
            res_log2 = n_res.bit_length() - 1
            n_keys = BAND * n_res * HEADS
            row = lax.broadcasted_iota(jnp.int32, (n_q, n_keys), 0)
            col = lax.broadcasted_iota(jnp.int32, (n_q, n_keys), 1)
            t = row >> heads_log2
            ok = (((col & (HEADS - 1)) == (row & (HEADS - 1)))
                  & (((col >> heads_log2) & (n_res - 1)) == (t & (dil - 1)))
                  & ((col >> (heads_log2 + res_log2)) >= (t >> dil_log2)))
            bc_refs[g][...] = jnp.where(ok, 0.0, -jnp.inf).astype(F32)
            row = lax.broadcasted_iota(jnp.int32, (n_q, n_q), 0)
            col = lax.broadcasted_iota(jnp.int32, (n_q, n_q), 1)
            dn = (row >> heads_log2) - (col >> heads_log2)
            ok = (((col & (HEADS - 1)) == (row & (HEADS - 1)))
                  & (dn >= 0) & ((dn & (dil - 1)) == 0))
            bn_ref[g] = jnp.where(ok, 0.0, -jnp.inf).astype(F32)

    contract_last = (((1,), (1,)), ((), ()))
    outs, lses = [], []
    for g, (_, dil) in enumerate(DIL_GROUPS):
        n_keys = BAND * min(dil, n_new) * HEADS
        q = q_refs[g][...].reshape(n_q, HEAD_DIM).astype(BF16)
        kn = kn_refs[g][...].reshape(n_q, HEAD_DIM).astype(BF16)
        vn = vn_refs[g][...].reshape(n_q, HEAD_DIM).astype(BF16)
        kc = ck_refs[g][...].reshape(n_keys, HEAD_DIM).astype(BF16)
        vc = cv_refs[g][...].reshape(n_keys, HEAD_DIM).astype(BF16)
        s = lax.dot_general(q, kc, contract_last, preferred_element_type=F32) * scale
        s = s + bc_refs[g][...]
        s_new = lax.dot_general(q, kn, contract_last, preferred_element_type=F32) * scale
        s_new = s_new + bn_ref[g]
        m = jnp.maximum(jnp.max(s, axis=-1, keepdims=True),
                        jnp.max(s_new, axis=-1, keepdims=True))
        p = jnp.exp(s - m)
        p_new = jnp.exp(s_new - m)
        den = jnp.sum(p, axis=-1, keepdims=True) + jnp.sum(p_new, axis=-1, keepdims=True)
        acc = (jnp.dot(p.astype(BF16), vc, preferred_element_type=F32)
               + jnp.dot(p_new.astype(BF16), vn, preferred_element_type=F32))
        outs.append(acc / den)
        lses.append(m + jnp.log(den))
    mx = jnp.maximum(jnp.maximum(lses[0], lses[1]), lses[2])
    ws = [jnp.exp(l - mx) for l in lses]
    tot = ws[0] + ws[1] + ws[2]
    o = (ws[0] * outs[0] + ws[1] * outs[1] + ws[2] * outs[2]) / tot
    o_ref[...] = o.reshape(n_new, HEADS, HEAD_DIM)


def _attn_sample(q_f32, kn_f32, vn_f32, caches_k, caches_v, n_batch, n_new):
    row_spec = pl.BlockSpec((n_new, HEADS, HEAD_DIM), lambda b: (b, 0, 0))
    in_specs = [row_spec] * 9
    args = list(q_f32) + list(kn_f32) + list(vn_f32)
    n_q = n_new * HEADS
    scratch = []
    for g, (win, dil) in enumerate(DIL_GROUPS):
        n_res = min(dil, n_new)
        for c in (caches_k[g], caches_v[g]):
            in_specs.append(pl.BlockSpec((None, BAND, n_res, HEADS, HEAD_DIM),
                                         lambda b: (b, 0, 0, 0, 0)))
            args.append(c.reshape(n_batch, win // dil, dil, HEADS, HEAD_DIM))
        scratch.append(pltpu.VMEM((n_q, BAND * n_res * HEADS), F32))
    scratch.append(pltpu.VMEM((N_DIL, n_q, n_q), F32))
    body = functools.partial(_attn_sample_body, n_new=n_new)
    return pl.pallas_call(
        body,
        grid=(n_batch,),
        in_specs=in_specs,
        out_specs=row_spec,
        out_shape=jax.ShapeDtypeStruct((n_batch * n_new, HEADS, HEAD_DIM), F32),
        scratch_shapes=scratch,
        compiler_params=pltpu.CompilerParams(
            dimension_semantics=("arbitrary",), vmem_limit_bytes=VMEM_LIMIT),
        name="attn_sample",
    )(*args)


def _final_body(*refs, tm, mix_rows, alpha, n_groups):
    zb_ref, gu_ref, vn_ref, za_ref, ga0_ref, ga1_ref, gb0_ref, gb1_ref = refs[:8]
    refs = refs[8:]
    if n_groups:
        og_refs = refs[:n_groups]
        lg_refs = refs[n_groups:2 * n_groups]
        refs = refs[2 * n_groups:]
    else:
        o_ref = refs[0]
        refs = refs[1:]
    (x_ref, wmix_ref, bmix_ref, woa_ref, wob_ref, wout_ref, lng_ref, lnb_ref,
     y_ref, ya_ref, yb_ref) = refs

    if n_groups:
        lses = [l_ref[...] for l_ref in lg_refs]
        mx = functools.reduce(jnp.maximum, lses)
        ws = [jnp.exp(l - mx) for l in lses]
        tot = functools.reduce(lambda a, b: a + b, ws)
        ws = [w / tot for w in ws]
        for h in range(HEADS):
            lanes = slice(h * HEAD_DIM, (h + 1) * HEAD_DIM)
            o = None
            for w, og_ref in zip(ws, og_refs):
                term = w[:, h * LSE_REP:h * LSE_REP + 1] * og_ref[:, lanes].astype(F32)
                o = term if o is None else o + term
            yb_ref[:, lanes] = (o * zb_ref[:, lanes].astype(F32)).astype(BF16)
    else:
        yb_ref[...] = (o_ref[...] * zb_ref[...].astype(F32)).astype(BF16)

    row = lax.broadcasted_iota(jnp.int32, (CHUNK, CHUNK), 0)
    col = lax.broadcasted_iota(jnp.int32, (CHUNK, CHUNK), 1)
    mix_log2 = mix_rows.bit_length() - 1
    causal = (row >= col) & ((row >> mix_log2) == (col >> mix_log2))
    for g in range(HEADS):
        lanes = slice(g * CHUNK, (g + 1) * CHUNK)
        w_c = jnp.where(causal, wmix_ref[g], 0.0).astype(BF16)
        for c in range(tm // CHUNK):
            rows = slice(c * CHUNK, (c + 1) * CHUNK)
            mixed = jnp.dot(w_c, vn_ref[rows, lanes], preferred_element_type=F32)
            mixed = mixed + bmix_ref[:, lanes]
            ya = gu_ref[rows, lanes].astype(F32) * mixed * za_ref[rows, lanes].astype(F32)
            ya_ref[rows, lanes] = ya.astype(BF16)

    proj_a = jnp.dot(ya_ref[...], woa_ref[...], preferred_element_type=F32)
    proj_b = jnp.dot(yb_ref[...], wob_ref[...], preferred_element_type=F32)
    ga = jnp.concatenate([ga0_ref[...], ga1_ref[...]], axis=1).astype(F32)
    gb = jnp.concatenate([gb0_ref[...], gb1_ref[...]], axis=1).astype(F32)
    merged = (ga * proj_a + gb * proj_b).astype(BF16)
    z = alpha * x_ref[...] + jnp.dot(merged, wout_ref[...], preferred_element_type=F32)
    y_ref[...] = _layer_norm_rows(z, lng_ref[...], lnb_ref[...])


def _final(h3d, o_parts, lse_parts, x, wmix, bmix, woa, wob, wout, ln_g, ln_b,
           mix_rows, alpha, name):
    n_rows = x.shape[0]
    tm = 256
    n_groups = len(lse_parts)
    hspec = lambda jj: pl.BlockSpec((None, tm, WIDTH), lambda i, jj=jj: (jj, i, 0))
    rowspec = lambda width: pl.BlockSpec((tm, width), lambda i: (i, 0))
    const = lambda shape: pl.BlockSpec(shape, lambda i: (0,) * len(shape),
                                       pipeline_mode=pl.Buffered(1))
    in_specs = [hspec(J_ZB), hspec(J_U), hspec(J_VA), hspec(J_ZA),
                hspec(J_GA), hspec(J_GA + 1), hspec(J_GB), hspec(J_GB + 1)]
    in_specs += [rowspec(WIDTH)] * len(o_parts)
    in_specs += [rowspec(HEADS * LSE_REP)] * n_groups
    in_specs += [rowspec(D_MODEL),
                 const((HEADS, CHUNK, CHUNK)), const((CHUNK, WIDTH)),
                 const((WIDTH, D_MODEL)), const((WIDTH, D_MODEL)), const((D_MODEL, D_MODEL)),
                 const((1, D_MODEL)), const((1, D_MODEL))]
    body = functools.partial(_final_body, tm=tm, mix_rows=mix_rows, alpha=np.float32(alpha),
                             n_groups=n_groups)
    return pl.pallas_call(
        body,
        grid=(n_rows // tm,),
        in_specs=in_specs,
        out_specs=pl.BlockSpec((tm, D_MODEL), lambda i: (i, 0)),
        out_shape=jax.ShapeDtypeStruct((n_rows, D_MODEL), F32),
        scratch_shapes=[pltpu.VMEM((tm, WIDTH), BF16), pltpu.VMEM((tm, WIDTH), BF16)],
        compiler_params=pltpu.CompilerParams(
            dimension_semantics=("arbitrary",), vmem_limit_bytes=VMEM_LIMIT),
        name=name,
    )(*([h3d] * 8), *o_parts, *lse_parts, x, wmix, bmix, woa, wob, wout, ln_g, ln_b)


def _rope_tables(pos):
    half = HEAD_DIM // 2
    inv = ROPE_THETA ** (-jnp.arange(0, half, dtype=F32) * 2.0 / HEAD_DIM)
    ang = pos.astype(F32)[:, None] * inv[None, :]
    cos = jnp.cos(ang)
    sin = jnp.sin(ang)
    return jnp.concatenate([cos, cos], axis=-1), jnp.concatenate([-sin, sin], axis=-1)


def kernel(x_prompt, x_sample, cache_k_w128, cache_v_w128, cache_k_w512, cache_v_w512,
           cache_k_w2048, cache_v_w2048, w_in, b_in, w_s, b_s, ln_v_g, ln_v_b,
           w_o_a, w_o_b, w_out, ln_g, ln_b):
    depth = w_in.shape[0]
    assert depth == 1, "single-layer step"
    batch, seq, _ = x_prompt.shape
    n_dec, n_new, _ = x_sample.shape
    assert seq == SEQ and n_new == 8 and x_prompt.shape[2] == D_MODEL
    caches_k = (cache_k_w128, cache_k_w512, cache_k_w2048)
    caches_v = (cache_v_w128, cache_v_w512, cache_v_w2048)
    for g, (win, _) in enumerate(DIL_GROUPS):
        assert caches_k[g].shape == (depth, n_dec, win, HEADS, HEAD_DIM)
    past_len = cache_k_w2048.shape[2]
    alpha = float(2 * depth) ** 0.25

    w_bf = w_in.reshape(D_MODEL, -1).astype(BF16)
    b2 = b_in.reshape(1, -1)
    lvg = ln_v_g.reshape(1, WIDTH)
    lvb = ln_v_b.reshape(1, WIDTH)
    lg = ln_g.reshape(1, D_MODEL)
    lb = ln_b.reshape(1, D_MODEL)
    ws = w_s.reshape(HEADS, CHUNK, CHUNK)
    bs = b_s.reshape(HEADS, CHUNK)
    woa = w_o_a.reshape(WIDTH, D_MODEL).astype(BF16)
    wob = w_o_b.reshape(WIDTH, D_MODEL).astype(BF16)
    wout = w_out.reshape(D_MODEL, D_MODEL).astype(BF16)

    cos_p, sin_p = _rope_tables(jnp.arange(seq, dtype=jnp.int32))
    pos_s = past_len + jnp.tile(jnp.arange(n_new, dtype=jnp.int32), n_dec)
    cos_s, sin_s = _rope_tables(pos_s)

    xp2 = x_prompt.reshape(batch * seq, D_MODEL)
    xs2 = x_sample.reshape(n_dec * n_new, D_MODEL)

    pouts = _inproj_prompt(xp2.astype(BF16), w_bf, b2, cos_p, sin_p, lvg, lvb)
    h3d = pouts[0]
    new_pk, new_pv = pouts[1:4], pouts[4:7]
    o_parts, lse_parts = [], []
    for g in range(N_DIL):
        o_g, lse_g = _attn_prompt_group(h3d, g, batch)
        o_parts.append(o_g)
        lse_parts.append(lse_g)
    bmix_p = jnp.repeat(bs.T, CHUNK, axis=1)
    yp = _final(h3d, o_parts, lse_parts, xp2, ws, bmix_p, woa, wob, wout, lg, lb,
                CHUNK, alpha, "final_prompt")

    souts = _inproj_sample(xs2.astype(BF16), w_bf, b2, cos_s, sin_s, lvg, lvb)
    hs3d = souts[0]
    q_s, kn_s, vn_s, gv_s = souts[1:4], souts[4:7], souts[7:10], souts[10]
    o_s = _attn_sample(q_s, kn_s, vn_s, caches_k, caches_v, n_dec, n_new)
    reps = CHUNK // n_new
    wmix_s = jnp.tile(ws[:, :n_new, :n_new], (1, reps, reps))
    bmix_s = jnp.repeat(jnp.tile(bs[:, :n_new], (1, reps)).T, CHUNK, axis=1)
    ys = _final(hs3d, [o_s.reshape(n_dec * n_new, WIDTH)], [], xs2, wmix_s, bmix_s, woa, wob, wout, lg, lb, n_new, alpha,
                "final_sample")

    new_p = tuple(a[None] for pair in zip(new_pk, new_pv) for a in pair)
    shp_s = lambda a: a.reshape(depth, n_dec, n_new, HEADS, HEAD_DIM)
    new_s = tuple(shp_s(a) for pair in zip(kn_s, vn_s) for a in pair)
    return ((yp.reshape(batch, seq, D_MODEL), ys.reshape(n_dec, n_new, D_MODEL))
            + new_p + new_s + (gv_s.reshape(depth, n_dec, n_new, WIDTH),))
```

```python
import functools

import numpy as np
import jax
import jax.numpy as jnp
from jax import lax
from jax.experimental import pallas as pl
from jax.experimental.pallas import tpu as pltpu

F32 = jnp.float32
BF16 = jnp.bfloat16

D_MODEL = 2048
HEAD_DIM = 128
HEADS = 8
DIL_GROUPS = ((128, 1), (512, 4), (2048, 16))
N_DIL = len(DIL_GROUPS)
BAND = 128
WIDTH = HEADS * HEAD_DIM
CHUNK = 128
SEQ = 2048
ROPE_THETA = 10000.0
LN_EPS = 1e-5
N_COL_TILES = 17
J_K0, J_V0, J_ZB, J_U, J_VA, J_ZA, J_GA, J_GB = 3, 6, 9, 10, 11, 12, 13, 15
EPILOGUE_ROWS = 64
MXU_COLS = 256
STAGE_CHUNK = 256
STAGE_HEADS = 4
STAGE_RESIDUES = 2
LSE_REP_LOG2 = 4
LSE_REP = 1 << LSE_REP_LOG2
VMEM_LIMIT = 56 * 1024 * 1024


def _gelu(x):
    return 0.5 * x * (1.0 + lax.erf(x * np.float32(np.sqrt(0.5))))


def _sigmoid(x):
    return 1.0 / (1.0 + jnp.exp(-x))


def _silu(x):
    return x * _sigmoid(x)


def _layer_norm_rows(x, g, b):
    mu = jnp.mean(x, axis=-1, keepdims=True)
    xc = x - mu
    var = jnp.mean(xc * xc, axis=-1, keepdims=True)
    return xc * lax.rsqrt(var + LN_EPS) * g + b


def _inproj_body(x_ref, w_ref, b_ref, cos_ref, sin_ref, lng_ref, lnb_ref, h_ref, *rest,
                 tm, f32_specs):
    f32_refs = rest[:len(f32_specs)]
    acc_ref, wbf_ref = rest[-2:]
    j = pl.program_id(0)
    i = pl.program_id(1)
    tiles_per_batch = SEQ // tm
    last_of_batch = i % tiles_per_batch == tiles_per_batch - 1

    @pl.when(i == 0)
    def _():
        def narrow(c, carry):
            rows = pl.ds(pl.multiple_of(c * MXU_COLS, MXU_COLS), MXU_COLS)
            wbf_ref[rows, :] = w_ref[rows, :].astype(BF16)
            return carry
        lax.fori_loop(0, D_MODEL // MXU_COLS, narrow, 0)

    def project(fn, emit_bf16=True):
        for c in range(WIDTH // MXU_COLS):
            cols = slice(c * MXU_COLS, (c + 1) * MXU_COLS)
            a = (jnp.dot(x_ref[...], wbf_ref[:, cols], preferred_element_type=F32)
                 + b_ref[:, cols])
            res = fn(a)
            acc_ref[:, cols] = res
            if emit_bf16:
                h_ref[:, cols] = res.astype(BF16)

    def rope(a):
        cos = cos_ref[...]
        sin = sin_ref[...]
        parts = []
        for h in range(MXU_COLS // HEAD_DIM):
            ah = a[:, h * HEAD_DIM:(h + 1) * HEAD_DIM]
            parts.append(ah * cos + pltpu.roll(ah, HEAD_DIM // 2, 1) * sin)
        return jnp.concatenate(parts, axis=1)

    @pl.when(j < J_V0)
    def _():
        project(rope)

    @pl.when((j >= J_V0) & (j < J_ZB))
    def _():
        project(lambda a: a)

    @pl.when((j == J_ZB) | (j == J_ZA))
    def _():
        project(_silu)

    @pl.when(j == J_U)
    def _():
        project(_gelu)

    @pl.when(j == J_VA)
    def _():
        project(_gelu, emit_bf16=False)

        def normalise(c, carry):
            rows = pl.ds(pl.multiple_of(c * EPILOGUE_ROWS, EPILOGUE_ROWS), EPILOGUE_ROWS)
            res = _layer_norm_rows(acc_ref[rows, :], lng_ref[...], lnb_ref[...])
            acc_ref[rows, :] = res
            h_ref[rows, :] = res.astype(BF16)
            return carry
        lax.fori_loop(0, tm // EPILOGUE_ROWS, normalise, 0)

    @pl.when(j >= J_GA)
    def _():
        project(_sigmoid)

    def scatter_heads(o_ref, row0, n_rows):
        for h in range(HEADS):
            o_ref[pl.ds(h, n_rows, stride=HEADS), :] = (
                acc_ref[row0:row0 + n_rows, h * HEAD_DIM:(h + 1) * HEAD_DIM])

    for (j_own, kind), o_ref in zip(f32_specs, f32_refs):
        if kind == 'heads':
            @pl.when(j == j_own)
            def _(o_ref=o_ref):
                scatter_heads(o_ref, 0, tm)
        elif kind == 'heads_last':
            @pl.when((j == j_own) & last_of_batch)
            def _(o_ref=o_ref):
                scatter_heads(o_ref, 0, tm)
        elif kind == 'heads_tail':
            @pl.when((j == j_own) & last_of_batch)
            def _(o_ref=o_ref):
                scatter_heads(o_ref, tm - 128, 128)
        elif kind == 'flat':
            @pl.when(j == j_own)
            def _(o_ref=o_ref):
                o_ref[...] = acc_ref[...]
        else:
            raise ValueError(kind)


def _sticky(j_own, first, last, idx_fn):
    def index_map(j, i):
        idx = idx_fn(i)
        return tuple(jnp.where(j < j_own, f, jnp.where(j > j_own, l, k))
                     for f, l, k in zip(first, last, idx))
    return index_map


def _inproj_call(x_bf, w_f32, b_in, cos_tab, sin_tab, ln_g, ln_b, *, tm, table_tiles,
                 f32_specs, f32_shapes, f32_blocks, name):
    n_rows = x_bf.shape[0]
    n_i = n_rows // tm
    in_specs = [
        pl.BlockSpec((tm, D_MODEL), lambda j, i: (i, 0)),
        pl.BlockSpec((D_MODEL, WIDTH), lambda j, i: (0, j)),
        pl.BlockSpec((1, WIDTH), lambda j, i: (0, j)),
        pl.BlockSpec((tm, HEAD_DIM), lambda j, i: (i % table_tiles, 0)),
        pl.BlockSpec((tm, HEAD_DIM), lambda j, i: (i % table_tiles, 0)),
        pl.BlockSpec((1, WIDTH), lambda j, i: (0, 0)),
        pl.BlockSpec((1, WIDTH), lambda j, i: (0, 0)),
    ]
    out_shapes = [jax.ShapeDtypeStruct((N_COL_TILES, n_rows, WIDTH), BF16)] + list(f32_shapes)
    out_specs = [pl.BlockSpec((None, tm, WIDTH), lambda j, i: (j, i, 0))] + list(f32_blocks)
    body = functools.partial(_inproj_body, tm=tm, f32_specs=tuple(f32_specs))
    return pl.pallas_call(
        body,
        grid=(N_COL_TILES, n_i),
        in_specs=in_specs,
        out_specs=out_specs,
        out_shape=out_shapes,
        scratch_shapes=[pltpu.VMEM((tm, WIDTH), F32), pltpu.VMEM((D_MODEL, WIDTH), BF16)],
        compiler_params=pltpu.CompilerParams(
            dimension_semantics=("arbitrary", "arbitrary"), vmem_limit_bytes=VMEM_LIMIT),
        name=name,
    )(x_bf, w_f32, b_in, cos_tab, sin_tab, ln_g, ln_b)


def _inproj_prompt(x_bf, w_f32, b_in, cos_tab, sin_tab, ln_g, ln_b):
    batch = x_bf.shape[0] // SEQ
    tm = 512
    tpb = SEQ // tm
    specs, shapes, blocks = [], [], []
    for base in (J_K0, J_V0):
        for g, (win, _) in enumerate(DIL_GROUPS):
            keep = min(win, SEQ)
            shapes.append(jax.ShapeDtypeStruct((batch * keep * HEADS, HEAD_DIM), F32))
            if keep < tm:
                specs.append((base + g, 'heads_tail'))
                idx_fn = lambda i: (i // tpb, 0)
                rows = keep
            elif keep == tm:
                specs.append((base + g, 'heads_last'))
                idx_fn = lambda i: (i // tpb, 0)
                rows = tm
            else:
                assert keep == SEQ
                specs.append((base + g, 'heads'))
                idx_fn = lambda i: (i, 0)
                rows = tm
            last = (batch * keep // rows - 1, 0)
            blocks.append(pl.BlockSpec((rows * HEADS, HEAD_DIM),
                                       _sticky(base + g, (0, 0), last, idx_fn)))
    return _inproj_call(x_bf, w_f32, b_in, cos_tab, sin_tab, ln_g, ln_b, tm=tm,
                        table_tiles=tpb, f32_specs=specs, f32_shapes=shapes,
                        f32_blocks=blocks, name="inproj_prompt")


def _inproj_sample(x_bf, w_f32, b_in, cos_tab, sin_tab, ln_g, ln_b):
    n_rows = x_bf.shape[0]
    tm = 256
    n_i = n_rows // tm
    specs, shapes, blocks = [], [], []
    for jo in range(J_ZB):
        specs.append((jo, 'heads'))
        shapes.append(jax.ShapeDtypeStruct((n_rows * HEADS, HEAD_DIM), F32))
        blocks.append(pl.BlockSpec((tm * HEADS, HEAD_DIM),
                                   _sticky(jo, (0, 0), (n_i - 1, 0), lambda i: (i, 0))))
    specs.append((J_VA, 'flat'))
    shapes.append(jax.ShapeDtypeStruct((n_rows, WIDTH), F32))
    blocks.append(pl.BlockSpec((tm, WIDTH),
                               _sticky(J_VA, (0, 0), (n_i - 1, 0), lambda i: (i, 0))))
    return _inproj_call(x_bf, w_f32, b_in, cos_tab, sin_tab, ln_g, ln_b, tm=tm,
                        table_tiles=n_i, f32_specs=specs, f32_shapes=shapes,
                        f32_blocks=blocks, name="inproj_sample")


def _lse_tile(cols):
    rows = cols[0].shape[0]
    lane_head = lax.broadcasted_iota(jnp.int32, (rows, HEADS * LSE_REP), 1) >> LSE_REP_LOG2
    tile = jnp.zeros((rows, HEADS * LSE_REP), F32)
    for h, c in enumerate(cols):
        tile = jnp.where(lane_head == h, c, tile)
    return tile


def _band_softmax(qkvs, off):
    win = qkvs[0][1].shape[0]
    scale = np.float32(HEAD_DIM ** -0.5)
    qi = lax.broadcasted_iota(jnp.int32, (BAND, win), 0)
    ki = lax.broadcasted_iota(jnp.int32, (BAND, win), 1)
    diff = qi - ki + off
    mask = (diff >= 0) & (diff <= BAND)
    scores = [lax.dot_general(q, k, (((1,), (1,)), ((), ())), preferred_element_type=F32)
              for q, k, _ in qkvs]
    probs, dens, lses = [], [], []
    for s in scores:
        s = jnp.where(mask, s * scale, -jnp.inf)
        m = jnp.max(s, axis=-1, keepdims=True)
        p = jnp.exp(s - m)
        den = jnp.sum(p, axis=-1, keepdims=True)
        probs.append(p.astype(BF16))
        dens.append(den)
        lses.append(m + jnp.log(den))
    outs = [jnp.dot(p, v, preferred_element_type=F32) for p, (_, _, v) in zip(probs, qkvs)]
    return [(o / den, lse) for o, den, lse in zip(outs, dens, lses)]


def _attn_prompt_body(q_ref, k_ref, v_ref, o_ref, lse_ref, *scratch, dil):
    sub_len = SEQ // dil
    n_blocks = sub_len // BAND
    nb_log2 = n_blocks.bit_length() - 1

    if dil == 1:
        def block(n, carry):
            q0 = pl.multiple_of(n * BAND, BAND)
            kb = jnp.maximum(n - 1, 0)
            k0 = pl.multiple_of(kb * BAND, BAND)
            off = (n - kb) * BAND
            head_lanes = [slice(h * HEAD_DIM, (h + 1) * HEAD_DIM) for h in range(HEADS)]
            res = _band_softmax([(q_ref[pl.ds(q0, BAND), lanes],
                                  k_ref[pl.ds(k0, 2 * BAND), lanes],
                                  v_ref[pl.ds(k0, 2 * BAND), lanes]) for lanes in head_lanes],
                                off)
            for lanes, (o, _) in zip(head_lanes, res):
                o_ref[pl.ds(q0, BAND), lanes] = o.astype(BF16)
            lse_ref[pl.ds(q0, BAND), :] = _lse_tile([lse for _, lse in res])
            return carry
        lax.fori_loop(0, n_blocks, block, 0)
        return

    qs_ref, ks_ref, vs_ref, os_ref, ls_ref = scratch
    m = STAGE_CHUNK // dil
    m_log2 = m.bit_length() - 1
    n_chunks = SEQ // STAGE_CHUNK
    row = lax.broadcasted_iota(jnp.int32, (STAGE_CHUNK, STAGE_CHUNK), 0)
    col = lax.broadcasted_iota(jnp.int32, (STAGE_CHUNK, STAGE_CHUNK), 1)
    to_staged = col == ((row & (m - 1)) * dil + (row >> m_log2))
    to_natural = row == ((col & (m - 1)) * dil + (col >> m_log2))
    perm = jnp.where(to_staged, 1.0, 0.0).astype(BF16)
    unperm = jnp.where(to_natural, 1.0, 0.0).astype(BF16)

    def staged_rows(c, r):
        return pl.ds(pl.multiple_of(r * sub_len + c * m, m), m)

    def gather_chunk(ref, c):
        return jnp.concatenate([ref[staged_rows(c, r), :] for r in range(dil)], axis=0)

    def natural_rows(c):
        return pl.ds(pl.multiple_of(c * STAGE_CHUNK, STAGE_CHUNK), STAGE_CHUNK)

    for hp in range(HEADS // STAGE_HEADS):
        pass_lanes = slice(hp * STAGE_HEADS * HEAD_DIM, (hp + 1) * STAGE_HEADS * HEAD_DIM)

        def stage(c, carry, pass_lanes=pass_lanes):
            for src, dst in ((q_ref, qs_ref), (k_ref, ks_ref), (v_ref, vs_ref)):
                y = jnp.dot(perm, src[natural_rows(c), pass_lanes],
                            preferred_element_type=F32).astype(BF16)
                for r in range(dil):
                    dst[staged_rows(c, r), :] = y[r * m:(r + 1) * m, :]
            return carry
        lax.fori_loop(0, n_chunks, stage, 0)

        def tile(t, carry, hp=hp):
            n = t & (n_blocks - 1)
            r0 = (t >> nb_log2) * STAGE_RESIDUES
            if n_blocks == 1:
                kb = 0
                off = 0
            else:
                kb = jnp.maximum(n - 1, 0)
                off = (n - kb) * BAND
            key_rows = min(2 * BAND, sub_len)
            head_lanes = [slice(hh * HEAD_DIM, (hh + 1) * HEAD_DIM) for hh in range(STAGE_HEADS)]
            items, dests = [], []
            for rr in range(STAGE_RESIDUES):
                base = (r0 + rr) * sub_len
                q_rows = pl.ds(pl.multiple_of(base + n * BAND, BAND), BAND)
                k_rows = pl.ds(pl.multiple_of(base + kb * BAND, BAND), key_rows)
                for hh, lanes in enumerate(head_lanes):
                    items.append((qs_ref[q_rows, lanes], ks_ref[k_rows, lanes],
                                  vs_ref[k_rows, lanes]))
                    dests.append((q_rows, lanes, hp * STAGE_HEADS + hh))
            for (q_rows, lanes, h), (o, lse) in zip(dests, _band_softmax(items, off)):
                os_ref[q_rows, lanes] = o.astype(BF16)
                ls_ref[q_rows, h * LSE_REP:(h + 1) * LSE_REP] = jnp.broadcast_to(
                    lse, (BAND, LSE_REP))
            return carry
        lax.fori_loop(0, dil * n_blocks // STAGE_RESIDUES, tile, 0)

        def unstage(c, carry, pass_lanes=pass_lanes):
            o_ref[natural_rows(c), pass_lanes] = jnp.dot(
                unperm, gather_chunk(os_ref, c), preferred_element_type=F32).astype(BF16)
            return carry
        lax.fori_loop(0, n_chunks, unstage, 0)

    def unstage_lse(c, carry):
        z = gather_chunk(ls_ref, c)
        hi = z.astype(BF16)
        rest = z - hi.astype(F32)
        mid = rest.astype(BF16)
        lo = (rest - mid.astype(F32)).astype(BF16)
        move = lambda piece: jnp.dot(unperm, piece, preferred_element_type=F32)
        lse_ref[natural_rows(c), :] = move(hi) + (move(mid) + move(lo))
        return carry
    lax.fori_loop(0, n_chunks, unstage_lse, 0)


def _attn_prompt_group(h3d, g, batch):
    _, dil = DIL_GROUPS[g]
    tile = (None, SEQ, WIDTH)
    in_specs = [
        pl.BlockSpec(tile, lambda b, g=g: (g, b, 0)),
        pl.BlockSpec(tile, lambda b, g=g: (J_K0 + g, b, 0)),
        pl.BlockSpec(tile, lambda b, g=g: (J_V0 + g, b, 0)),
    ]
    scratch = [] if dil == 1 else (
        [pltpu.VMEM((SEQ, STAGE_HEADS * HEAD_DIM), BF16)] * 4
        + [pltpu.VMEM((SEQ, HEADS * LSE_REP), F32)])
    return pl.pallas_call(
        functools.partial(_attn_prompt_body, dil=dil),
        grid=(batch,),
        in_specs=in_specs,
        out_specs=[pl.BlockSpec((SEQ, WIDTH), lambda b: (b, 0)),
                   pl.BlockSpec((SEQ, HEADS * LSE_REP), lambda b: (b, 0))],
        out_shape=[jax.ShapeDtypeStruct((batch * SEQ, WIDTH), BF16),
                   jax.ShapeDtypeStruct((batch * SEQ, HEADS * LSE_REP), F32)],
        scratch_shapes=scratch,
        compiler_params=pltpu.CompilerParams(
            dimension_semantics=("arbitrary",), vmem_limit_bytes=VMEM_LIMIT),
        name="attn_prompt_g%d" % g,
    )(h3d, h3d, h3d)


def _attn_sample_body(q0_ref, q1_ref, q2_ref, kn0_ref, kn1_ref, kn2_ref,
                      vn0_ref, vn1_ref, vn2_ref,
                      ck0_ref, cv0_ref, ck1_ref, cv1_ref, ck2_ref, cv2_ref, o_ref,
                      bc0_ref, bc1_ref, bc2_ref, bn_ref, *, n_new):
    scale = np.float32(HEAD_DIM ** -0.5)
    q_refs = (q0_ref, q1_ref, q2_ref)
    kn_refs = (kn0_ref, kn1_ref, kn2_ref)
    vn_refs = (vn0_ref, vn1_ref, vn2_ref)
    ck_refs = (ck0_ref, ck1_ref, ck2_ref)
    cv_refs = (cv0_ref, cv1_ref, cv2_ref)
    bc_refs = (bc0_ref, bc1_ref, bc2_ref)
    n_q = n_new * HEADS
    heads_log2 = HEADS.bit_length() - 1

    @pl.when(pl.program_id(0) == 0)
    def _():
        for g, (_, dil) in enumerate(DIL_GROUPS):
            n_res = min(dil, n_new)
            dil_log2 = dil.bit_length() - 1
            res_log2 = n_res.bit_length() - 1
            n_keys = BAND * n_res * HEADS
            row = lax.broadcasted_iota(jnp.int32, (n_q, n_keys), 0)
            col = lax.broadcasted_iota(jnp.int32, (n_q, n_keys), 1)
            t = row >> heads_log2
            ok = (((col & (HEADS - 1)) == (row & (HEADS - 1)))
                  & (((col >> heads_log2) & (n_res - 1)) == (t & (dil - 1)))
                  & ((col >> (heads_log2 + res_log2)) >= (t >> dil_log2)))
            bc_refs[g][...] = jnp.where(ok, 0.0, -jnp.inf).astype(F32)
            row = lax.broadcasted_iota(jnp.int32, (n_q, n_q), 0)
            col = lax.broadcasted_iota(jnp.int32, (n_q, n_q), 1)
            dn = (row >> heads_log2) - (col >> heads_log2)
            ok = (((col & (HEADS - 1)) == (row & (HEADS - 1)))
                  & (dn >= 0) & ((dn & (dil - 1)) == 0))
            bn_ref[g] = jnp.where(ok, 0.0, -jnp.inf).astype(F32)

    contract_last = (((1,), (1,)), ((), ()))
    outs, lses = [], []
    for g, (_, dil) in enumerate(DIL_GROUPS):
        n_keys = BAND * min(dil, n_new) * HEADS
        q = q_refs[g][...].reshape(n_q, HEAD_DIM).astype(BF16)
        kn = kn_refs[g][...].reshape(n_q, HEAD_DIM).astype(BF16)
        vn = vn_refs[g][...].reshape(n_q, HEAD_DIM).astype(BF16)
        kc = ck_refs[g][...].reshape(n_keys, HEAD_DIM).astype(BF16)
        vc = cv_refs[g][...].reshape(n_keys, HEAD_DIM).astype(BF16)
        s = lax.dot_general(q, kc, contract_last, preferred_element_type=F32) * scale
        s = s + bc_refs[g][...]
        s_new = lax.dot_general(q, kn, contract_last, preferred_element_type=F32) * scale
        s_new = s_new + bn_ref[g]
        m = jnp.maximum(jnp.max(s, axis=-1, keepdims=True),
                        jnp.max(s_new, axis=-1, keepdims=True))
        p = jnp.exp(s - m)
        p_new = jnp.exp(s_new - m)
        den = jnp.sum(p, axis=-1, keepdims=True) + jnp.sum(p_new, axis=-1, keepdims=True)
        acc = (jnp.dot(p.astype(BF16), vc, preferred_element_type=F32)
               + jnp.dot(p_new.astype(BF16), vn, preferred_element_type=F32))
        outs.append(acc / den)
        lses.append(m + jnp.log(den))
    mx = jnp.maximum(jnp.maximum(lses[0], lses[1]), lses[2])
    ws = [jnp.exp(l - mx) for l in lses]
    tot = ws[0] + ws[1] + ws[2]
    o = (ws[0] * outs[0] + ws[1] * outs[1] + ws[2] * outs[2]) / tot
    o_ref[...] = o.reshape(n_new, HEADS, HEAD_DIM)


def _attn_sample(q_f32, kn_f32, vn_f32, caches_k, caches_v, n_batch, n_new):
    row_spec = pl.BlockSpec((n_new, HEADS, HEAD_DIM), lambda b: (b, 0, 0))
    in_specs = [row_spec] * 9
    args = list(q_f32) + list(kn_f32) + list(vn_f32)
    n_q = n_new * HEADS
    scratch = []
    for g, (win, dil) in enumerate(DIL_GROUPS):
        n_res = min(dil, n_new)
        for c in (caches_k[g], caches_v[g]):
            in_specs.append(pl.BlockSpec((None, BAND, n_res, HEADS, HEAD_DIM),
                                         lambda b: (b, 0, 0, 0, 0)))
            args.append(c.reshape(n_batch, win // dil, dil, HEADS, HEAD_DIM))
        scratch.append(pltpu.VMEM((n_q, BAND * n_res * HEADS), F32))
    scratch.append(pltpu.VMEM((N_DIL, n_q, n_q), F32))
    body = functools.partial(_attn_sample_body, n_new=n_new)
    return pl.pallas_call(
        body,
        grid=(n_batch,),
        in_specs=in_specs,
        out_specs=row_spec,
        out_shape=jax.ShapeDtypeStruct((n_batch * n_new, HEADS, HEAD_DIM), F32),
        scratch_shapes=scratch,
        compiler_params=pltpu.CompilerParams(
            dimension_semantics=("arbitrary",), vmem_limit_bytes=VMEM_LIMIT),
        name="attn_sample",
    )(*args)


def _final_body(*refs, tm, mix_rows, alpha, n_groups):
    zb_ref, gu_ref, vn_ref, za_ref, ga0_ref, ga1_ref, gb0_ref, gb1_ref = refs[:8]
    refs = refs[8:]
    if n_groups:
        og_refs = refs[:n_groups]
        lg_refs = refs[n_groups:2 * n_groups]
        refs = refs[2 * n_groups:]
    else:
        o_ref = refs[0]
        refs = refs[1:]
    (x_ref, wmix_ref, bmix_ref, woa_ref, wob_ref, wout_ref, lng_ref, lnb_ref,
     y_ref, ya_ref, yb_ref) = refs

    if n_groups:
        lses = [l_ref[...] for l_ref in lg_refs]
        mx = functools.reduce(jnp.maximum, lses)
        ws = [jnp.exp(l - mx) for l in lses]
        tot = functools.reduce(lambda a, b: a + b, ws)
        ws = [w / tot for w in ws]
        for h in range(HEADS):
            lanes = slice(h * HEAD_DIM, (h + 1) * HEAD_DIM)
            o = None
            for w, og_ref in zip(ws, og_refs):
                term = w[:, h * LSE_REP:h * LSE_REP + 1] * og_ref[:, lanes].astype(F32)
                o = term if o is None else o + term
            yb_ref[:, lanes] = (o * zb_ref[:, lanes].astype(F32)).astype(BF16)
    else:
        yb_ref[...] = (o_ref[...] * zb_ref[...].astype(F32)).astype(BF16)

    row = lax.broadcasted_iota(jnp.int32, (CHUNK, CHUNK), 0)
    col = lax.broadcasted_iota(jnp.int32, (CHUNK, CHUNK), 1)
    mix_log2 = mix_rows.bit_length() - 1
    causal = (row >= col) & ((row >> mix_log2) == (col >> mix_log2))
    for g in range(HEADS):
        lanes = slice(g * CHUNK, (g + 1) * CHUNK)
        w_c = jnp.where(causal, wmix_ref[g], 0.0).astype(BF16)
        for c in range(tm // CHUNK):
            rows = slice(c * CHUNK, (c + 1) * CHUNK)
            mixed = jnp.dot(w_c, vn_ref[rows, lanes], preferred_element_type=F32)
            mixed = mixed + bmix_ref[:, lanes]
            ya = gu_ref[rows, lanes].astype(F32) * mixed * za_ref[rows, lanes].astype(F32)
            ya_ref[rows, lanes] = ya.astype(BF16)

    proj_a = jnp.dot(ya_ref[...], woa_ref[...], preferred_element_type=F32)
    proj_b = jnp.dot(yb_ref[...], wob_ref[...], preferred_element_type=F32)
    ga = jnp.concatenate([ga0_ref[...], ga1_ref[...]], axis=1).astype(F32)
    gb = jnp.concatenate([gb0_ref[...], gb1_ref[...]], axis=1).astype(F32)
    merged = (ga * proj_a + gb * proj_b).astype(BF16)
    z = alpha * x_ref[...] + jnp.dot(merged, wout_ref[...], preferred_element_type=F32)
    y_ref[...] = _layer_norm_rows(z, lng_ref[...], lnb_ref[...])


def _final(h3d, o_parts, lse_parts, x, wmix, bmix, woa, wob, wout, ln_g, ln_b,
           mix_rows, alpha, name):
    n_rows = x.shape[0]
    tm = 256
    n_groups = len(lse_parts)
    hspec = lambda jj: pl.BlockSpec((None, tm, WIDTH), lambda i, jj=jj: (jj, i, 0))
    rowspec = lambda width: pl.BlockSpec((tm, width), lambda i: (i, 0))
    const = lambda shape: pl.BlockSpec(shape, lambda i: (0,) * len(shape),
                                       pipeline_mode=pl.Buffered(1))
    in_specs = [hspec(J_ZB), hspec(J_U), hspec(J_VA), hspec(J_ZA),
                hspec(J_GA), hspec(J_GA + 1), hspec(J_GB), hspec(J_GB + 1)]
    in_specs += [rowspec(WIDTH)] * len(o_parts)
    in_specs += [rowspec(HEADS * LSE_REP)] * n_groups
    in_specs += [rowspec(D_MODEL),
                 const((HEADS, CHUNK, CHUNK)), const((CHUNK, WIDTH)),
                 const((WIDTH, D_MODEL)), const((WIDTH, D_MODEL)), const((D_MODEL, D_MODEL)),
                 const((1, D_MODEL)), const((1, D_MODEL))]
    body = functools.partial(_final_body, tm=tm, mix_rows=mix_rows, alpha=np.float32(alpha),
                             n_groups=n_groups)
    return pl.pallas_call(
        body,
        grid=(n_rows // tm,),
        in_specs=in_specs,
        out_specs=pl.BlockSpec((tm, D_MODEL), lambda i: (i, 0)),
        out_shape=jax.ShapeDtypeStruct((n_rows, D_MODEL), F32),
        scratch_shapes=[pltpu.VMEM((tm, WIDTH), BF16), pltpu.VMEM((tm, WIDTH), BF16)],
        compiler_params=pltpu.CompilerParams(
            dimension_semantics=("arbitrary",), vmem_limit_bytes=VMEM_LIMIT),
        name=name,
    )(*([h3d] * 8), *o_parts, *lse_parts, x, wmix, bmix, woa, wob, wout, ln_g, ln_b)


def _rope_tables(pos):
    half = HEAD_DIM // 2
    inv = ROPE_THETA ** (-jnp.arange(0, half, dtype=F32) * 2.0 / HEAD_DIM)
    ang = pos.astype(F32)[:, None] * inv[None, :]
    cos = jnp.cos(ang)
    sin = jnp.sin(ang)
    return jnp.concatenate([cos, cos], axis=-1), jnp.concatenate([-sin, sin], axis=-1)


def kernel(x_prompt, x_sample, cache_k_w128, cache_v_w128, cache_k_w512, cache_v_w512,
           cache_k_w2048, cache_v_w2048, w_in, b_in, w_s, b_s, ln_v_g, ln_v_b,
           w_o_a, w_o_b, w_out, ln_g, ln_b):
    depth = w_in.shape[0]
    assert depth == 1, "single-layer step"
    batch, seq, _ = x_prompt.shape
    n_dec, n_new, _ = x_sample.shape
    assert seq == SEQ and n_new == 8 and x_prompt.shape[2] == D_MODEL
    caches_k = (cache_k_w128, cache_k_w512, cache_k_w2048)
    caches_v = (cache_v_w128, cache_v_w512, cache_v_w2048)
    for g, (win, _) in enumerate(DIL_GROUPS):
        assert caches_k[g].shape == (depth, n_dec, win, HEADS, HEAD_DIM)
    past_len = cache_k_w2048.shape[2]
    alpha = float(2 * depth) ** 0.25

    w_f32 = w_in.reshape(D_MODEL, -1)
    b2 = b_in.reshape(1, -1)
    lvg = ln_v_g.reshape(1, WIDTH)
    lvb = ln_v_b.reshape(1, WIDTH)
    lg = ln_g.reshape(1, D_MODEL)
    lb = ln_b.reshape(1, D_MODEL)
    ws = w_s.reshape(HEADS, CHUNK, CHUNK)
    bs = b_s.reshape(HEADS, CHUNK)
    woa = w_o_a.reshape(WIDTH, D_MODEL).astype(BF16)
    wob = w_o_b.reshape(WIDTH, D_MODEL).astype(BF16)
    wout = w_out.reshape(D_MODEL, D_MODEL).astype(BF16)

    cos_p, sin_p = _rope_tables(jnp.arange(seq, dtype=jnp.int32))
    pos_s = past_len + jnp.tile(jnp.arange(n_new, dtype=jnp.int32), n_dec)
    cos_s, sin_s = _rope_tables(pos_s)

    xp2 = x_prompt.reshape(batch * seq, D_MODEL)
    xs2 = x_sample.reshape(n_dec * n_new, D_MODEL)

    pouts = _inproj_prompt(xp2.astype(BF16), w_f32, b2, cos_p, sin_p, lvg, lvb)
    h3d = pouts[0]
    new_pk, new_pv = pouts[1:4], pouts[4:7]
    o_parts, lse_parts = [], []
    for g in range(N_DIL):
        o_g, lse_g = _attn_prompt_group(h3d, g, batch)
        o_parts.append(o_g)
        lse_parts.append(lse_g)
    bmix_p = jnp.repeat(bs.T, CHUNK, axis=1)
    yp = _final(h3d, o_parts, lse_parts, xp2, ws, bmix_p, woa, wob, wout, lg, lb,
                CHUNK, alpha, "final_prompt")

    souts = _inproj_sample(xs2.astype(BF16), w_f32, b2, cos_s, sin_s, lvg, lvb)
    hs3d = souts[0]
    heads3d = lambda a: a.reshape(-1, HEADS, HEAD_DIM)
    q_s, kn_s, vn_s = ([heads3d(a) for a in souts[lo:lo + N_DIL]] for lo in (1, 4, 7))
    gv_s = souts[10]
    o_s = _attn_sample(q_s, kn_s, vn_s, caches_k, caches_v, n_dec, n_new)
    reps = CHUNK // n_new
    wmix_s = jnp.tile(ws[:, :n_new, :n_new], (1, reps, reps))
    bmix_s = jnp.repeat(jnp.tile(bs[:, :n_new], (1, reps)).T, CHUNK, axis=1)
    ys = _final(hs3d, [o_s.reshape(n_dec * n_new, WIDTH)], [], xs2, wmix_s, bmix_s, woa, wob, wout, lg, lb, n_new, alpha,
                "final_sample")

    shp_p = lambda a: a.reshape(depth, batch, -1, HEADS, HEAD_DIM)
    new_p = tuple(shp_p(a) for pair in zip(new_pk, new_pv) for a in pair)
    shp_s = lambda a: a.reshape(depth, n_dec, n_new, HEADS, HEAD_DIM)
    new_s = tuple(shp_s(a) for pair in zip(kn_s, vn_s) for a in pair)
    return ((yp.reshape(batch, seq, D_MODEL), ys.reshape(n_dec, n_new, D_MODEL))
            + new_p + new_s + (gv_s.reshape(depth, n_dec, n_new, WIDTH),))
```

```python
import functools

import numpy as np
import jax
import jax.numpy as jnp
from jax import lax
from jax.experimental import pallas as pl
from jax.experimental.pallas import tpu as pltpu

F32 = jnp.float32
BF16 = jnp.bfloat16

D_MODEL = 2048
HEAD_DIM = 128
HEADS = 8
DIL_GROUPS = ((128, 1), (512, 4), (2048, 16))
N_DIL = len(DIL_GROUPS)
BAND = 128
WIDTH = HEADS * HEAD_DIM
CHUNK = 128
SEQ = 2048
ROPE_THETA = 10000.0
LN_EPS = 1e-5
N_COL_TILES = 17
J_K0, J_V0, J_ZB, J_U, J_VA, J_ZA, J_GA, J_GB = 3, 6, 9, 10, 11, 12, 13, 15
EPILOGUE_ROWS = 64
MXU_COLS = 256
STAGE_CHUNK = 256
STAGE_HEADS = 4
STAGE_RESIDUES = 2
LSE_REP_LOG2 = 4
LSE_REP = 1 << LSE_REP_LOG2
VMEM_LIMIT = 56 * 1024 * 1024


def _gelu(x):
    return 0.5 * x * (1.0 + lax.erf(x * np.float32(np.sqrt(0.5))))


def _sigmoid(x):
    return 1.0 / (1.0 + jnp.exp(-x))


def _silu(x):
    return x * _sigmoid(x)


def _layer_norm_rows(x, g, b):
    mu = jnp.mean(x, axis=-1, keepdims=True)
    xc = x - mu
    var = jnp.mean(xc * xc, axis=-1, keepdims=True)
    return xc * lax.rsqrt(var + LN_EPS) * g + b


def _inproj_body(x_ref, w_ref, b_ref, cos_ref, sin_ref, lng_ref, lnb_ref, h_ref, *rest,
                 tm, f32_specs):
    f32_refs = rest[:len(f32_specs)]
    acc_ref, wbf_ref = rest[-2:]
    j = pl.program_id(0)
    i = pl.program_id(1)
    tiles_per_batch = SEQ // tm
    last_of_batch = i % tiles_per_batch == tiles_per_batch - 1

    @pl.when(i == 0)
    def _():
        def narrow(c, carry):
            rows = pl.ds(pl.multiple_of(c * MXU_COLS, MXU_COLS), MXU_COLS)
            wbf_ref[rows, :] = w_ref[rows, :].astype(BF16)
            return carry
        lax.fori_loop(0, D_MODEL // MXU_COLS, narrow, 0)

    def project(fn, emit_bf16=True):
        for c in range(WIDTH // MXU_COLS):
            cols = slice(c * MXU_COLS, (c + 1) * MXU_COLS)
            a = (jnp.dot(x_ref[...], wbf_ref[:, cols], preferred_element_type=F32)
                 + b_ref[:, cols])
            res = fn(a)
            acc_ref[:, cols] = res
            if emit_bf16:
                h_ref[:, cols] = res.astype(BF16)

    def rope(a):
        cos = cos_ref[...]
        sin = sin_ref[...]
        parts = []
        for h in range(MXU_COLS // HEAD_DIM):
            ah = a[:, h * HEAD_DIM:(h + 1) * HEAD_DIM]
            parts.append(ah * cos + pltpu.roll(ah, HEAD_DIM // 2, 1) * sin)
        return jnp.concatenate(parts, axis=1)

    @pl.when(j < J_V0)
    def _():
        project(rope)

    @pl.when((j >= J_V0) & (j < J_ZB))
    def _():
        project(lambda a: a)

    @pl.when((j == J_ZB) | (j == J_ZA))
    def _():
        project(_silu)

    @pl.when(j == J_U)
    def _():
        project(_gelu)

    @pl.when(j == J_VA)
    def _():
        project(_gelu, emit_bf16=False)

        def normalise(c, carry):
            rows = pl.ds(pl.multiple_of(c * EPILOGUE_ROWS, EPILOGUE_ROWS), EPILOGUE_ROWS)
            res = _layer_norm_rows(acc_ref[rows, :], lng_ref[...], lnb_ref[...])
            acc_ref[rows, :] = res
            h_ref[rows, :] = res.astype(BF16)
            return carry
        lax.fori_loop(0, tm // EPILOGUE_ROWS, normalise, 0)

    @pl.when(j >= J_GA)
    def _():
        project(_sigmoid)

    def scatter_heads(o_ref, row0, n_rows):
        for h in range(HEADS):
            o_ref[pl.ds(h, n_rows, stride=HEADS), :] = (
                acc_ref[row0:row0 + n_rows, h * HEAD_DIM:(h + 1) * HEAD_DIM])

    for (j_own, kind), o_ref in zip(f32_specs, f32_refs):
        if kind == 'heads':
            @pl.when(j == j_own)
            def _(o_ref=o_ref):
                scatter_heads(o_ref, 0, tm)
        elif kind == 'heads_below':
            @pl.when(j < j_own)
            def _(o_ref=o_ref):
                scatter_heads(o_ref, 0, tm)
        elif kind == 'heads_last':
            @pl.when((j == j_own) & last_of_batch)
            def _(o_ref=o_ref):
                scatter_heads(o_ref, 0, tm)
        elif kind == 'heads_tail':
            @pl.when((j == j_own) & last_of_batch)
            def _(o_ref=o_ref):
                scatter_heads(o_ref, tm - 128, 128)
        elif kind == 'flat':
            @pl.when(j == j_own)
            def _(o_ref=o_ref):
                o_ref[...] = acc_ref[...]
        else:
            raise ValueError(kind)


def _sticky(j_own, first, last, idx_fn):
    def index_map(j, i):
        idx = idx_fn(i)
        return tuple(jnp.where(j < j_own, f, jnp.where(j > j_own, l, k))
                     for f, l, k in zip(first, last, idx))
    return index_map


def _inproj_call(x_bf, w_f32, b_in, cos_tab, sin_tab, ln_g, ln_b, *, tm, table_tiles,
                 f32_specs, f32_shapes, f32_blocks, name):
    n_rows = x_bf.shape[0]
    n_i = n_rows // tm
    in_specs = [
        pl.BlockSpec((tm, D_MODEL), lambda j, i: (i, 0)),
        pl.BlockSpec((D_MODEL, WIDTH), lambda j, i: (0, j)),
        pl.BlockSpec((1, WIDTH), lambda j, i: (0, j)),
        pl.BlockSpec((tm, HEAD_DIM), lambda j, i: (i % table_tiles, 0)),
        pl.BlockSpec((tm, HEAD_DIM), lambda j, i: (i % table_tiles, 0)),
        pl.BlockSpec((1, WIDTH), lambda j, i: (0, 0)),
        pl.BlockSpec((1, WIDTH), lambda j, i: (0, 0)),
    ]
    out_shapes = [jax.ShapeDtypeStruct((N_COL_TILES, n_rows, WIDTH), BF16)] + list(f32_shapes)
    out_specs = [pl.BlockSpec((None, tm, WIDTH), lambda j, i: (j, i, 0))] + list(f32_blocks)
    body = functools.partial(_inproj_body, tm=tm, f32_specs=tuple(f32_specs))
    return pl.pallas_call(
        body,
        grid=(N_COL_TILES, n_i),
        in_specs=in_specs,
        out_specs=out_specs,
        out_shape=out_shapes,
        scratch_shapes=[pltpu.VMEM((tm, WIDTH), F32), pltpu.VMEM((D_MODEL, WIDTH), BF16)],
        compiler_params=pltpu.CompilerParams(
            dimension_semantics=("arbitrary", "arbitrary"), vmem_limit_bytes=VMEM_LIMIT),
        name=name,
    )(x_bf, w_f32, b_in, cos_tab, sin_tab, ln_g, ln_b)


def _inproj_prompt(x_bf, w_f32, b_in, cos_tab, sin_tab, ln_g, ln_b):
    batch = x_bf.shape[0] // SEQ
    tm = 512
    tpb = SEQ // tm
    specs, shapes, blocks = [], [], []
    for base in (J_K0, J_V0):
        for g, (win, _) in enumerate(DIL_GROUPS):
            keep = min(win, SEQ)
            shapes.append(jax.ShapeDtypeStruct((batch * keep * HEADS, HEAD_DIM), F32))
            if keep < tm:
                specs.append((base + g, 'heads_tail'))
                idx_fn = lambda i: (i // tpb, 0)
                rows = keep
            elif keep == tm:
                specs.append((base + g, 'heads_last'))
                idx_fn = lambda i: (i // tpb, 0)
                rows = tm
            else:
                assert keep == SEQ
                specs.append((base + g, 'heads'))
                idx_fn = lambda i: (i, 0)
                rows = tm
            last = (batch * keep // rows - 1, 0)
            blocks.append(pl.BlockSpec((rows * HEADS, HEAD_DIM),
                                       _sticky(base + g, (0, 0), last, idx_fn)))
    return _inproj_call(x_bf, w_f32, b_in, cos_tab, sin_tab, ln_g, ln_b, tm=tm,
                        table_tiles=tpb, f32_specs=specs, f32_shapes=shapes,
                        f32_blocks=blocks, name="inproj_prompt")


def _inproj_sample(x_bf, w_f32, b_in, cos_tab, sin_tab, ln_g, ln_b):
    n_rows = x_bf.shape[0]
    tm = 512
    n_i = n_rows // tm
    specs = [(J_ZB, 'heads_below')]
    shapes = [jax.ShapeDtypeStruct((J_ZB, n_rows * HEADS, HEAD_DIM), F32)]
    blocks = [pl.BlockSpec(
        (None, tm * HEADS, HEAD_DIM),
        lambda j, i: (jnp.minimum(j, J_ZB - 1), jnp.where(j < J_ZB, i, n_i - 1), 0))]
    specs.append((J_VA, 'flat'))
    shapes.append(jax.ShapeDtypeStruct((n_rows, WIDTH), F32))
    blocks.append(pl.BlockSpec((tm, WIDTH),
                               _sticky(J_VA, (0, 0), (n_i - 1, 0), lambda i: (i, 0))))
    return _inproj_call(x_bf, w_f32, b_in, cos_tab, sin_tab, ln_g, ln_b, tm=tm,
                        table_tiles=n_i, f32_specs=specs, f32_shapes=shapes,
                        f32_blocks=blocks, name="inproj_sample")


def _lse_tile(cols):
    rows = cols[0].shape[0]
    lane_head = lax.broadcasted_iota(jnp.int32, (rows, HEADS * LSE_REP), 1) >> LSE_REP_LOG2
    tile = jnp.zeros((rows, HEADS * LSE_REP), F32)
    for h, c in enumerate(cols):
        tile = jnp.where(lane_head == h, c, tile)
    return tile


def _band_softmax(qkvs, off):
    win = qkvs[0][1].shape[0]
    scale = np.float32(HEAD_DIM ** -0.5)
    qi = lax.broadcasted_iota(jnp.int32, (BAND, win), 0)
    ki = lax.broadcasted_iota(jnp.int32, (BAND, win), 1)
    diff = qi - ki + off
    mask = (diff >= 0) & (diff <= BAND)
    scores = [lax.dot_general(q, k, (((1,), (1,)), ((), ())), preferred_element_type=F32)
              for q, k, _ in qkvs]
    probs, dens, lses = [], [], []
    for s in scores:
        s = jnp.where(mask, s * scale, -jnp.inf)
        m = jnp.max(s, axis=-1, keepdims=True)
        p = jnp.exp(s - m)
        den = jnp.sum(p, axis=-1, keepdims=True)
        probs.append(p.astype(BF16))
        dens.append(den)
        lses.append(m + jnp.log(den))
    outs = [jnp.dot(p, v, preferred_element_type=F32) for p, (_, _, v) in zip(probs, qkvs)]
    return [(o / den, lse) for o, den, lse in zip(outs, dens, lses)]


def _attn_prompt_body(q_ref, k_ref, v_ref, o_ref, lse_ref, *scratch, dil):
    sub_len = SEQ // dil
    n_blocks = sub_len // BAND
    nb_log2 = n_blocks.bit_length() - 1

    if dil == 1:
        def block(n, carry):
            q0 = pl.multiple_of(n * BAND, BAND)
            kb = jnp.maximum(n - 1, 0)
            k0 = pl.multiple_of(kb * BAND, BAND)
            off = (n - kb) * BAND
            head_lanes = [slice(h * HEAD_DIM, (h + 1) * HEAD_DIM) for h in range(HEADS)]
            res = _band_softmax([(q_ref[pl.ds(q0, BAND), lanes],
                                  k_ref[pl.ds(k0, 2 * BAND), lanes],
                                  v_ref[pl.ds(k0, 2 * BAND), lanes]) for lanes in head_lanes],
                                off)
            for lanes, (o, _) in zip(head_lanes, res):
                o_ref[pl.ds(q0, BAND), lanes] = o.astype(BF16)
            lse_ref[pl.ds(q0, BAND), :] = _lse_tile([lse for _, lse in res])
            return carry
        lax.fori_loop(0, n_blocks, block, 0)
        return

    qs_ref, ks_ref, vs_ref, os_ref, ls_ref = scratch
    m = STAGE_CHUNK // dil
    m_log2 = m.bit_length() - 1
    n_chunks = SEQ // STAGE_CHUNK
    row = lax.broadcasted_iota(jnp.int32, (STAGE_CHUNK, STAGE_CHUNK), 0)
    col = lax.broadcasted_iota(jnp.int32, (STAGE_CHUNK, STAGE_CHUNK), 1)
    to_staged = col == ((row & (m - 1)) * dil + (row >> m_log2))
    to_natural = row == ((col & (m - 1)) * dil + (col >> m_log2))
    perm = jnp.where(to_staged, 1.0, 0.0).astype(BF16)
    unperm = jnp.where(to_natural, 1.0, 0.0).astype(BF16)

    def staged_rows(c, r):
        return pl.ds(pl.multiple_of(r * sub_len + c * m, m), m)

    def gather_chunk(ref, c):
        return jnp.concatenate([ref[staged_rows(c, r), :] for r in range(dil)], axis=0)

    def natural_rows(c):
        return pl.ds(pl.multiple_of(c * STAGE_CHUNK, STAGE_CHUNK), STAGE_CHUNK)

    for hp in range(HEADS // STAGE_HEADS):
        pass_lanes = slice(hp * STAGE_HEADS * HEAD_DIM, (hp + 1) * STAGE_HEADS * HEAD_DIM)

        def stage(c, carry, pass_lanes=pass_lanes):
            for src, dst in ((q_ref, qs_ref), (k_ref, ks_ref), (v_ref, vs_ref)):
                y = jnp.dot(perm, src[natural_rows(c), pass_lanes],
                            preferred_element_type=F32).astype(BF16)
                for r in range(dil):
                    dst[staged_rows(c, r), :] = y[r * m:(r + 1) * m, :]
            return carry
        lax.fori_loop(0, n_chunks, stage, 0, unroll=2)

        def tile(t, carry, hp=hp):
            n = t & (n_blocks - 1)
            r0 = (t >> nb_log2) * STAGE_RESIDUES
            if n_blocks == 1:
                kb = 0
                off = 0
            else:
                kb = jnp.maximum(n - 1, 0)
                off = (n - kb) * BAND
            key_rows = min(2 * BAND, sub_len)
            head_lanes = [slice(hh * HEAD_DIM, (hh + 1) * HEAD_DIM) for hh in range(STAGE_HEADS)]
            items, dests = [], []
            for rr in range(STAGE_RESIDUES):
                base = (r0 + rr) * sub_len
                q_rows = pl.ds(pl.multiple_of(base + n * BAND, BAND), BAND)
                k_rows = pl.ds(pl.multiple_of(base + kb * BAND, BAND), key_rows)
                for hh, lanes in enumerate(head_lanes):
                    items.append((qs_ref[q_rows, lanes], ks_ref[k_rows, lanes],
                                  vs_ref[k_rows, lanes]))
                    dests.append((q_rows, lanes, hp * STAGE_HEADS + hh))
            for (q_rows, lanes, h), (o, lse) in zip(dests, _band_softmax(items, off)):
                os_ref[q_rows, lanes] = o.astype(BF16)
                ls_ref[q_rows, h * LSE_REP:(h + 1) * LSE_REP] = jnp.broadcast_to(
                    lse, (BAND, LSE_REP))
            return carry
        lax.fori_loop(0, dil * n_blocks // STAGE_RESIDUES, tile, 0)

        def unstage(c, carry, pass_lanes=pass_lanes):
            o_ref[natural_rows(c), pass_lanes] = jnp.dot(
                unperm, gather_chunk(os_ref, c), preferred_element_type=F32).astype(BF16)
            return carry
        lax.fori_loop(0, n_chunks, unstage, 0, unroll=2)

    def unstage_lse(c, carry):
        z = gather_chunk(ls_ref, c)
        hi = z.astype(BF16)
        rest = z - hi.astype(F32)
        mid = rest.astype(BF16)
        lo = (rest - mid.astype(F32)).astype(BF16)
        move = lambda piece: jnp.dot(unperm, piece, preferred_element_type=F32)
        lse_ref[natural_rows(c), :] = move(hi) + (move(mid) + move(lo))
        return carry
    lax.fori_loop(0, n_chunks, unstage_lse, 0, unroll=2)


def _attn_prompt_group(h3d, g, batch):
    _, dil = DIL_GROUPS[g]
    tile = (None, SEQ, WIDTH)
    in_specs = [
        pl.BlockSpec(tile, lambda b, g=g: (g, b, 0)),
        pl.BlockSpec(tile, lambda b, g=g: (J_K0 + g, b, 0)),
        pl.BlockSpec(tile, lambda b, g=g: (J_V0 + g, b, 0)),
    ]
    scratch = [] if dil == 1 else (
        [pltpu.VMEM((SEQ, STAGE_HEADS * HEAD_DIM), BF16)] * 4
        + [pltpu.VMEM((SEQ, HEADS * LSE_REP), F32)])
    return pl.pallas_call(
        functools.partial(_attn_prompt_body, dil=dil),
        grid=(batch,),
        in_specs=in_specs,
        out_specs=[pl.BlockSpec((SEQ, WIDTH), lambda b: (b, 0)),
                   pl.BlockSpec((SEQ, HEADS * LSE_REP), lambda b: (b, 0))],
        out_shape=[jax.ShapeDtypeStruct((batch * SEQ, WIDTH), BF16),
                   jax.ShapeDtypeStruct((batch * SEQ, HEADS * LSE_REP), F32)],
        scratch_shapes=scratch,
        compiler_params=pltpu.CompilerParams(
            dimension_semantics=("arbitrary",), vmem_limit_bytes=VMEM_LIMIT),
        name="attn_prompt_g%d" % g,
    )(h3d, h3d, h3d)


def _attn_sample_body(q0_ref, q1_ref, q2_ref, kn0_ref, kn1_ref, kn2_ref,
                      vn0_ref, vn1_ref, vn2_ref,
                      ck0_ref, cv0_ref, ck1_ref, cv1_ref, ck2_ref, cv2_ref, o_ref,
                      nk0_ref, nk1_ref, nk2_ref, nv0_ref, nv1_ref, nv2_ref,
                      bc0_ref, bc1_ref, bc2_ref, bn_ref, *, n_new):
    scale = np.float32(HEAD_DIM ** -0.5)
    q_refs = (q0_ref, q1_ref, q2_ref)
    kn_refs = (kn0_ref, kn1_ref, kn2_ref)
    vn_refs = (vn0_ref, vn1_ref, vn2_ref)
    ck_refs = (ck0_ref, ck1_ref, ck2_ref)
    cv_refs = (cv0_ref, cv1_ref, cv2_ref)
    bc_refs = (bc0_ref, bc1_ref, bc2_ref)
    n_q = n_new * HEADS
    heads_log2 = HEADS.bit_length() - 1

    @pl.when(pl.program_id(0) == 0)
    def _():
        for g, (_, dil) in enumerate(DIL_GROUPS):
            n_res = min(dil, n_new)
            dil_log2 = dil.bit_length() - 1
            res_log2 = n_res.bit_length() - 1
            n_keys = BAND * n_res * HEADS
            row = lax.broadcasted_iota(jnp.int32, (n_q, n_keys), 0)
            col = lax.broadcasted_iota(jnp.int32, (n_q, n_keys), 1)
            t = row >> heads_log2
            ok = (((col & (HEADS - 1)) == (row & (HEADS - 1)))
                  & (((col >> heads_log2) & (n_res - 1)) == (t & (dil - 1)))
                  & ((col >> (heads_log2 + res_log2)) >= (t >> dil_log2)))
            bc_refs[g][...] = jnp.where(ok, 0.0, -jnp.inf).astype(F32)
            row = lax.broadcasted_iota(jnp.int32, (n_q, n_q), 0)
            col = lax.broadcasted_iota(jnp.int32, (n_q, n_q), 1)
            dn = (row >> heads_log2) - (col >> heads_log2)
            ok = (((col & (HEADS - 1)) == (row & (HEADS - 1)))
                  & (dn >= 0) & ((dn & (dil - 1)) == 0))
            bn_ref[g] = jnp.where(ok, 0.0, -jnp.inf).astype(F32)

    contract_last = (((1,), (1,)), ((), ()))
    outs, lses = [], []
    for g, (_, dil) in enumerate(DIL_GROUPS):
        n_keys = BAND * min(dil, n_new) * HEADS
        q = q_refs[g][...].reshape(n_q, HEAD_DIM).astype(BF16)
        kn = kn_refs[g][...].reshape(n_q, HEAD_DIM).astype(BF16)
        vn = vn_refs[g][...].reshape(n_q, HEAD_DIM).astype(BF16)
        kc = ck_refs[g][...].reshape(n_keys, HEAD_DIM).astype(BF16)
        vc = cv_refs[g][...].reshape(n_keys, HEAD_DIM).astype(BF16)
        s = lax.dot_general(q, kc, contract_last, preferred_element_type=F32) * scale
        s = s + bc_refs[g][...]
        s_new = lax.dot_general(q, kn, contract_last, preferred_element_type=F32) * scale
        s_new = s_new + bn_ref[g]
        m = jnp.maximum(jnp.max(s, axis=-1, keepdims=True),
                        jnp.max(s_new, axis=-1, keepdims=True))
        p = jnp.exp(s - m)
        p_new = jnp.exp(s_new - m)
        den = jnp.sum(p, axis=-1, keepdims=True) + jnp.sum(p_new, axis=-1, keepdims=True)
        acc = (jnp.dot(p.astype(BF16), vc, preferred_element_type=F32)
               + jnp.dot(p_new.astype(BF16), vn, preferred_element_type=F32))
        outs.append(acc / den)
        lses.append(m + jnp.log(den))
    mx = jnp.maximum(jnp.maximum(lses[0], lses[1]), lses[2])
    ws = [jnp.exp(l - mx) for l in lses]
    tot = ws[0] + ws[1] + ws[2]
    o = (ws[0] * outs[0] + ws[1] * outs[1] + ws[2] * outs[2]) / tot
    o_ref[...] = o.reshape(n_new, HEADS, HEAD_DIM)
    for dst, src in zip((nk0_ref, nk1_ref, nk2_ref, nv0_ref, nv1_ref, nv2_ref),
                        kn_refs + vn_refs):
        dst[...] = src[...]


def _attn_sample(qkv, caches_k, caches_v, n_batch, n_new):
    row_block = (n_new, HEADS, HEAD_DIM)
    in_specs = [pl.BlockSpec((None,) + row_block, lambda b, k=k: (k, b, 0, 0))
                for k in range(3 * N_DIL)]
    args = [qkv] * (3 * N_DIL)
    n_q = n_new * HEADS
    scratch = []
    for g, (win, dil) in enumerate(DIL_GROUPS):
        n_res = min(dil, n_new)
        for c in (caches_k[g], caches_v[g]):
            in_specs.append(pl.BlockSpec((None, BAND, n_res, HEADS, HEAD_DIM),
                                         lambda b: (b, 0, 0, 0, 0)))
            args.append(c.reshape(n_batch, win // dil, dil, HEADS, HEAD_DIM))
        scratch.append(pltpu.VMEM((n_q, BAND * n_res * HEADS), F32))
    scratch.append(pltpu.VMEM((N_DIL, n_q, n_q), F32))
    row_spec = pl.BlockSpec(row_block, lambda b: (b, 0, 0))
    n_out = 1 + 2 * N_DIL
    body = functools.partial(_attn_sample_body, n_new=n_new)
    return pl.pallas_call(
        body,
        grid=(n_batch,),
        in_specs=in_specs,
        out_specs=[row_spec] * n_out,
        out_shape=[jax.ShapeDtypeStruct((n_batch * n_new,) + row_block[1:], F32)] * n_out,
        scratch_shapes=scratch,
        compiler_params=pltpu.CompilerParams(
            dimension_semantics=("arbitrary",), vmem_limit_bytes=VMEM_LIMIT),
        name="attn_sample",
    )(*args)


def _final_body(*refs, tm, mix_rows, alpha, n_groups):
    zb_ref, gu_ref, vn_ref, za_ref, ga0_ref, ga1_ref, gb0_ref, gb1_ref = refs[:8]
    refs = refs[8:]
    if n_groups:
        og_refs = refs[:n_groups]
        lg_refs = refs[n_groups:2 * n_groups]
        refs = refs[2 * n_groups:]
    else:
        o_ref = refs[0]
        refs = refs[1:]
    (x_ref, wmix_ref, bmix_ref, woa_ref, wob_ref, wout_ref, lng_ref, lnb_ref,
     y_ref, ya_ref, yb_ref) = refs

    if n_groups:
        lses = [l_ref[...] for l_ref in lg_refs]
        mx = functools.reduce(jnp.maximum, lses)
        ws = [jnp.exp(l - mx) for l in lses]
        tot = functools.reduce(lambda a, b: a + b, ws)
        ws = [w / tot for w in ws]
        for h in range(HEADS):
            lanes = slice(h * HEAD_DIM, (h + 1) * HEAD_DIM)
            o = None
            for w, og_ref in zip(ws, og_refs):
                term = w[:, h * LSE_REP:h * LSE_REP + 1] * og_ref[:, lanes].astype(F32)
                o = term if o is None else o + term
            yb_ref[:, lanes] = (o * zb_ref[:, lanes].astype(F32)).astype(BF16)
    else:
        yb_ref[...] = (o_ref[...] * zb_ref[...].astype(F32)).astype(BF16)

    row = lax.broadcasted_iota(jnp.int32, (CHUNK, CHUNK), 0)
    col = lax.broadcasted_iota(jnp.int32, (CHUNK, CHUNK), 1)
    mix_log2 = mix_rows.bit_length() - 1
    causal = (row >= col) & ((row >> mix_log2) == (col >> mix_log2))
    for g in range(HEADS):
        lanes = slice(g * CHUNK, (g + 1) * CHUNK)
        w_c = jnp.where(causal, wmix_ref[g], 0.0).astype(BF16)
        for c in range(tm // CHUNK):
            rows = slice(c * CHUNK, (c + 1) * CHUNK)
            mixed = jnp.dot(w_c, vn_ref[rows, lanes], preferred_element_type=F32)
            mixed = mixed + bmix_ref[:, lanes]
            ya = gu_ref[rows, lanes].astype(F32) * mixed * za_ref[rows, lanes].astype(F32)
            ya_ref[rows, lanes] = ya.astype(BF16)

    proj_a = jnp.dot(ya_ref[...], woa_ref[...], preferred_element_type=F32)
    proj_b = jnp.dot(yb_ref[...], wob_ref[...], preferred_element_type=F32)
    ga = jnp.concatenate([ga0_ref[...], ga1_ref[...]], axis=1).astype(F32)
    gb = jnp.concatenate([gb0_ref[...], gb1_ref[...]], axis=1).astype(F32)
    merged = (ga * proj_a + gb * proj_b).astype(BF16)
    z = alpha * x_ref[...] + jnp.dot(merged, wout_ref[...], preferred_element_type=F32)
    y_ref[...] = _layer_norm_rows(z, lng_ref[...], lnb_ref[...])


def _final(h3d, o_parts, lse_parts, x, wmix, bmix, woa, wob, wout, ln_g, ln_b,
           mix_rows, alpha, name):
    n_rows = x.shape[0]
    tm = 256
    n_groups = len(lse_parts)
    hspec = lambda jj: pl.BlockSpec((None, tm, WIDTH), lambda i, jj=jj: (jj, i, 0))
    rowspec = lambda width: pl.BlockSpec((tm, width), lambda i: (i, 0))
    const = lambda shape: pl.BlockSpec(shape, lambda i: (0,) * len(shape),
                                       pipeline_mode=pl.Buffered(1))
    in_specs = [hspec(J_ZB), hspec(J_U), hspec(J_VA), hspec(J_ZA),
                hspec(J_GA), hspec(J_GA + 1), hspec(J_GB), hspec(J_GB + 1)]
    in_specs += [rowspec(WIDTH)] * len(o_parts)
    in_specs += [rowspec(HEADS * LSE_REP)] * n_groups
    in_specs += [rowspec(D_MODEL),
                 const((HEADS, CHUNK, CHUNK)), const((CHUNK, WIDTH)),
                 const((WIDTH, D_MODEL)), const((WIDTH, D_MODEL)), const((D_MODEL, D_MODEL)),
                 const((1, D_MODEL)), const((1, D_MODEL))]
    body = functools.partial(_final_body, tm=tm, mix_rows=mix_rows, alpha=np.float32(alpha),
                             n_groups=n_groups)
    return pl.pallas_call(
        body,
        grid=(n_rows // tm,),
        in_specs=in_specs,
        out_specs=pl.BlockSpec((tm, D_MODEL), lambda i: (i, 0)),
        out_shape=jax.ShapeDtypeStruct((n_rows, D_MODEL), F32),
        scratch_shapes=[pltpu.VMEM((tm, WIDTH), BF16), pltpu.VMEM((tm, WIDTH), BF16)],
        compiler_params=pltpu.CompilerParams(
            dimension_semantics=("arbitrary",), vmem_limit_bytes=VMEM_LIMIT),
        name=name,
    )(*([h3d] * 8), *o_parts, *lse_parts, x, wmix, bmix, woa, wob, wout, ln_g, ln_b)


def _rope_tables(pos):
    half = HEAD_DIM // 2
    inv = ROPE_THETA ** (-jnp.arange(0, half, dtype=F32) * 2.0 / HEAD_DIM)
    ang = pos.astype(F32)[:, None] * inv[None, :]
    cos = jnp.cos(ang)
    sin = jnp.sin(ang)
    return jnp.concatenate([cos, cos], axis=-1), jnp.concatenate([-sin, sin], axis=-1)


def kernel(x_prompt, x_sample, cache_k_w128, cache_v_w128, cache_k_w512, cache_v_w512,
           cache_k_w2048, cache_v_w2048, w_in, b_in, w_s, b_s, ln_v_g, ln_v_b,
           w_o_a, w_o_b, w_out, ln_g, ln_b):
    depth = w_in.shape[0]
    assert depth == 1, "single-layer step"
    batch, seq, _ = x_prompt.shape
    n_dec, n_new, _ = x_sample.shape
    assert seq == SEQ and n_new == 8 and x_prompt.shape[2] == D_MODEL
    caches_k = (cache_k_w128, cache_k_w512, cache_k_w2048)
    caches_v = (cache_v_w128, cache_v_w512, cache_v_w2048)
    for g, (win, _) in enumerate(DIL_GROUPS):
        assert caches_k[g].shape == (depth, n_dec, win, HEADS, HEAD_DIM)
    past_len = cache_k_w2048.shape[2]
    alpha = float(2 * depth) ** 0.25

    w_f32 = w_in.reshape(D_MODEL, -1)
    b2 = b_in.reshape(1, -1)
    lvg = ln_v_g.reshape(1, WIDTH)
    lvb = ln_v_b.reshape(1, WIDTH)
    lg = ln_g.reshape(1, D_MODEL)
    lb = ln_b.reshape(1, D_MODEL)
    ws = w_s.reshape(HEADS, CHUNK, CHUNK)
    bs = b_s.reshape(HEADS, CHUNK)
    woa = w_o_a.reshape(WIDTH, D_MODEL).astype(BF16)
    wob = w_o_b.reshape(WIDTH, D_MODEL).astype(BF16)
    wout = w_out.reshape(D_MODEL, D_MODEL).astype(BF16)

    cos_p, sin_p = _rope_tables(jnp.arange(seq, dtype=jnp.int32))
    pos_s = past_len + jnp.tile(jnp.arange(n_new, dtype=jnp.int32), n_dec)
    cos_s, sin_s = _rope_tables(pos_s)

    xp2 = x_prompt.reshape(batch * seq, D_MODEL)
    xs2 = x_sample.reshape(n_dec * n_new, D_MODEL)

    pouts = _inproj_prompt(xp2.astype(BF16), w_f32, b2, cos_p, sin_p, lvg, lvb)
    h3d = pouts[0]
    new_pk, new_pv = pouts[1:4], pouts[4:7]
    o_parts, lse_parts = [], []
    for g in range(N_DIL):
        o_g, lse_g = _attn_prompt_group(h3d, g, batch)
        o_parts.append(o_g)
        lse_parts.append(lse_g)
    bmix_p = jnp.repeat(bs.T, CHUNK, axis=1)
    yp = _final(h3d, o_parts, lse_parts, xp2, ws, bmix_p, woa, wob, wout, lg, lb,
                CHUNK, alpha, "final_prompt")

    souts = _inproj_sample(xs2.astype(BF16), w_f32, b2, cos_s, sin_s, lvg, lvb)
    hs3d, qkv_s, gv_s = souts
    qkv_s = qkv_s.reshape(3 * N_DIL, n_dec * n_new, HEADS, HEAD_DIM)
    aouts = _attn_sample(qkv_s, caches_k, caches_v, n_dec, n_new)
    o_s, kn_s, vn_s = aouts[0], aouts[1:1 + N_DIL], aouts[1 + N_DIL:]
    reps = CHUNK // n_new
    wmix_s = jnp.tile(ws[:, :n_new, :n_new], (1, reps, reps))
    bmix_s = jnp.repeat(jnp.tile(bs[:, :n_new], (1, reps)).T, CHUNK, axis=1)
    ys = _final(hs3d, [o_s.reshape(n_dec * n_new, WIDTH)], [], xs2, wmix_s, bmix_s, woa, wob, wout, lg, lb, n_new, alpha,
                "final_sample")

    shp_p = lambda a: a.reshape(depth, batch, -1, HEADS, HEAD_DIM)
    new_p = tuple(shp_p(a) for pair in zip(new_pk, new_pv) for a in pair)
    shp_s = lambda a: a.reshape(depth, n_dec, n_new, HEADS, HEAD_DIM)
    new_s = tuple(shp_s(a) for pair in zip(kn_s, vn_s) for a in pair)
    return ((yp.reshape(batch, seq, D_MODEL), ys.reshape(n_dec, n_new, D_MODEL))
            + new_p + new_s + (gv_s.reshape(depth, n_dec, n_new, WIDTH),))
```

```python
import functools

import numpy as np
import jax
import jax.numpy as jnp
from jax import lax
from jax.experimental import pallas as pl
from jax.experimental.pallas import tpu as pltpu

F32 = jnp.float32
BF16 = jnp.bfloat16

D_MODEL = 2048
HEAD_DIM = 128
HEADS = 8
DIL_GROUPS = ((128, 1), (512, 4), (2048, 16))
N_DIL = len(DIL_GROUPS)
BAND = 128
WIDTH = HEADS * HEAD_DIM
CHUNK = 128
SEQ = 2048
ROPE_THETA = 10000.0
LN_EPS = 1e-5
N_COL_TILES = 17
J_K0, J_V0, J_ZB, J_U, J_VA, J_ZA, J_GA, J_GB = 3, 6, 9, 10, 11, 12, 13, 15
EPILOGUE_ROWS = 64
MXU_COLS = 256
STAGE_CHUNK = 256
STAGE_HEADS = 4
STAGE_RESIDUES = 2
LSE_REP_LOG2 = 4
LSE_REP = 1 << LSE_REP_LOG2
VMEM_LIMIT = 56 * 1024 * 1024


def _gelu(x):
    return 0.5 * x * (1.0 + lax.erf(x * np.float32(np.sqrt(0.5))))


def _sigmoid(x):
    return 0.5 * (jnp.tanh(0.5 * x) + 1.0)


def _silu(x):
    return x * _sigmoid(x)


def _layer_norm_rows(x, g, b):
    mu = jnp.mean(x, axis=-1, keepdims=True)
    xc = x - mu
    var = jnp.mean(xc * xc, axis=-1, keepdims=True)
    return xc * lax.rsqrt(var + LN_EPS) * g + b


def _inproj_body(x_ref, w_ref, b_ref, cos_ref, sin_ref, lng_ref, lnb_ref, h_ref, *rest,
                 tm, f32_specs):
    f32_refs = rest[:len(f32_specs)]
    acc_ref, wbf_ref = rest[-2:]
    j = pl.program_id(0)
    i = pl.program_id(1)
    tiles_per_batch = SEQ // tm
    last_of_batch = i % tiles_per_batch == tiles_per_batch - 1

    @pl.when(i == 0)
    def _():
        def narrow(c, carry):
            rows = pl.ds(pl.multiple_of(c * MXU_COLS, MXU_COLS), MXU_COLS)
            wbf_ref[rows, :] = w_ref[rows, :].astype(BF16)
            return carry
        lax.fori_loop(0, D_MODEL // MXU_COLS, narrow, 0)

    def project(fn, emit_bf16=True):
        for c in range(WIDTH // MXU_COLS):
            cols = slice(c * MXU_COLS, (c + 1) * MXU_COLS)
            a = (jnp.dot(x_ref[...], wbf_ref[:, cols], preferred_element_type=F32)
                 + b_ref[:, cols])
            res = fn(a)
            acc_ref[:, cols] = res
            if emit_bf16:
                h_ref[:, cols] = res.astype(BF16)

    def rope(a):
        cos = cos_ref[...]
        sin = sin_ref[...]
        parts = []
        for h in range(MXU_COLS // HEAD_DIM):
            ah = a[:, h * HEAD_DIM:(h + 1) * HEAD_DIM]
            parts.append(ah * cos + pltpu.roll(ah, HEAD_DIM // 2, 1) * sin)
        return jnp.concatenate(parts, axis=1)

    @pl.when(j < J_V0)
    def _():
        project(rope)

    @pl.when((j >= J_V0) & (j < J_ZB))
    def _():
        project(lambda a: a)

    @pl.when((j == J_ZB) | (j == J_ZA))
    def _():
        project(_silu)

    @pl.when(j == J_U)
    def _():
        project(_gelu)

    @pl.when(j == J_VA)
    def _():
        project(_gelu, emit_bf16=False)

        def normalise(c, carry):
            rows = pl.ds(pl.multiple_of(c * EPILOGUE_ROWS, EPILOGUE_ROWS), EPILOGUE_ROWS)
            res = _layer_norm_rows(acc_ref[rows, :], lng_ref[...], lnb_ref[...])
            acc_ref[rows, :] = res
            h_ref[rows, :] = res.astype(BF16)
            return carry
        lax.fori_loop(0, tm // EPILOGUE_ROWS, normalise, 0)

    @pl.when(j >= J_GA)
    def _():
        project(_sigmoid)

    def scatter_heads(o_ref, row0, n_rows):
        for h in range(HEADS):
            o_ref[pl.ds(h, n_rows, stride=HEADS), :] = (
                acc_ref[row0:row0 + n_rows, h * HEAD_DIM:(h + 1) * HEAD_DIM])

    for (j_own, kind), o_ref in zip(f32_specs, f32_refs):
        if kind == 'heads':
            @pl.when(j == j_own)
            def _(o_ref=o_ref):
                scatter_heads(o_ref, 0, tm)
        elif kind == 'heads_below':
            @pl.when(j < j_own)
            def _(o_ref=o_ref):
                scatter_heads(o_ref, 0, tm)
        elif kind == 'heads_last':
            @pl.when((j == j_own) & last_of_batch)
            def _(o_ref=o_ref):
                scatter_heads(o_ref, 0, tm)
        elif kind == 'heads_tail':
            @pl.when((j == j_own) & last_of_batch)
            def _(o_ref=o_ref):
                scatter_heads(o_ref, tm - 128, 128)
        elif kind == 'flat':
            @pl.when(j == j_own)
            def _(o_ref=o_ref):
                o_ref[...] = acc_ref[...]
        else:
            raise ValueError(kind)


def _sticky(j_own, first, last, idx_fn):
    def index_map(j, i):
        idx = idx_fn(i)
        return tuple(jnp.where(j < j_own, f, jnp.where(j > j_own, l, k))
                     for f, l, k in zip(first, last, idx))
    return index_map


def _inproj_call(x_bf, w_f32, b_in, cos_tab, sin_tab, ln_g, ln_b, *, tm, table_tiles,
                 f32_specs, f32_shapes, f32_blocks, name):
    n_rows = x_bf.shape[0]
    n_i = n_rows // tm
    in_specs = [
        pl.BlockSpec((tm, D_MODEL), lambda j, i: (i, 0)),
        pl.BlockSpec((D_MODEL, WIDTH), lambda j, i: (0, j)),
        pl.BlockSpec((1, WIDTH), lambda j, i: (0, j)),
        pl.BlockSpec((tm, HEAD_DIM), lambda j, i: (i % table_tiles, 0)),
        pl.BlockSpec((tm, HEAD_DIM), lambda j, i: (i % table_tiles, 0)),
        pl.BlockSpec((1, WIDTH), lambda j, i: (0, 0)),
        pl.BlockSpec((1, WIDTH), lambda j, i: (0, 0)),
    ]
    out_shapes = [jax.ShapeDtypeStruct((N_COL_TILES, n_rows, WIDTH), BF16)] + list(f32_shapes)
    out_specs = [pl.BlockSpec((None, tm, WIDTH), lambda j, i: (j, i, 0))] + list(f32_blocks)
    body = functools.partial(_inproj_body, tm=tm, f32_specs=tuple(f32_specs))
    return pl.pallas_call(
        body,
        grid=(N_COL_TILES, n_i),
        in_specs=in_specs,
        out_specs=out_specs,
        out_shape=out_shapes,
        scratch_shapes=[pltpu.VMEM((tm, WIDTH), F32), pltpu.VMEM((D_MODEL, WIDTH), BF16)],
        compiler_params=pltpu.CompilerParams(
            dimension_semantics=("arbitrary", "arbitrary"), vmem_limit_bytes=VMEM_LIMIT),
        name=name,
    )(x_bf, w_f32, b_in, cos_tab, sin_tab, ln_g, ln_b)


def _inproj_prompt(x_bf, w_f32, b_in, cos_tab, sin_tab, ln_g, ln_b):
    batch = x_bf.shape[0] // SEQ
    tm = 512
    tpb = SEQ // tm
    specs, shapes, blocks = [], [], []
    for base in (J_K0, J_V0):
        for g, (win, _) in enumerate(DIL_GROUPS):
            keep = min(win, SEQ)
            shapes.append(jax.ShapeDtypeStruct((batch * keep * HEADS, HEAD_DIM), F32))
            if keep < tm:
                specs.append((base + g, 'heads_tail'))
                idx_fn = lambda i: (i // tpb, 0)
                rows = keep
            elif keep == tm:
                specs.append((base + g, 'heads_last'))
                idx_fn = lambda i: (i // tpb, 0)
                rows = tm
            else:
                assert keep == SEQ
                specs.append((base + g, 'heads'))
                idx_fn = lambda i: (i, 0)
                rows = tm
            last = (batch * keep // rows - 1, 0)
            blocks.append(pl.BlockSpec((rows * HEADS, HEAD_DIM),
                                       _sticky(base + g, (0, 0), last, idx_fn)))
    return _inproj_call(x_bf, w_f32, b_in, cos_tab, sin_tab, ln_g, ln_b, tm=tm,
                        table_tiles=tpb, f32_specs=specs, f32_shapes=shapes,
                        f32_blocks=blocks, name="inproj_prompt")


def _inproj_sample(x_bf, w_f32, b_in, cos_tab, sin_tab, ln_g, ln_b):
    n_rows = x_bf.shape[0]
    tm = 512
    n_i = n_rows // tm
    specs = [(J_ZB, 'heads_below')]
    shapes = [jax.ShapeDtypeStruct((J_ZB, n_rows * HEADS, HEAD_DIM), F32)]
    blocks = [pl.BlockSpec(
        (None, tm * HEADS, HEAD_DIM),
        lambda j, i: (jnp.minimum(j, J_ZB - 1), jnp.where(j < J_ZB, i, n_i - 1), 0))]
    specs.append((J_VA, 'flat'))
    shapes.append(jax.ShapeDtypeStruct((n_rows, WIDTH), F32))
    blocks.append(pl.BlockSpec((tm, WIDTH),
                               _sticky(J_VA, (0, 0), (n_i - 1, 0), lambda i: (i, 0))))
    return _inproj_call(x_bf, w_f32, b_in, cos_tab, sin_tab, ln_g, ln_b, tm=tm,
                        table_tiles=n_i, f32_specs=specs, f32_shapes=shapes,
                        f32_blocks=blocks, name="inproj_sample")


def _lse_tile(cols):
    rows = cols[0].shape[0]
    lane_head = lax.broadcasted_iota(jnp.int32, (rows, HEADS * LSE_REP), 1) >> LSE_REP_LOG2
    tile = jnp.zeros((rows, HEADS * LSE_REP), F32)
    for h, c in enumerate(cols):
        tile = jnp.where(lane_head == h, c, tile)
    return tile


def _band_softmax(qkvs, off, min_key=None):
    win = qkvs[0][1].shape[0]
    scale = np.float32(HEAD_DIM ** -0.5)
    qi = lax.broadcasted_iota(jnp.int32, (BAND, win), 0)
    ki = lax.broadcasted_iota(jnp.int32, (BAND, win), 1)
    diff = qi - ki + off
    mask = (diff >= 0) & (diff <= BAND)
    if min_key is not None:
        mask = mask & (ki >= min_key)
    scores = [lax.dot_general(q, k, (((1,), (1,)), ((), ())), preferred_element_type=F32)
              for q, k, _ in qkvs]
    probs, dens, lses = [], [], []
    for s in scores:
        s = jnp.where(mask, s * scale, -jnp.inf)
        m = jnp.max(s, axis=-1, keepdims=True)
        p = jnp.exp(s - m)
        den = jnp.sum(p, axis=-1, keepdims=True)
        probs.append(p.astype(BF16))
        dens.append(den)
        lses.append(m + jnp.log(den))
    outs = [jnp.dot(p, v, preferred_element_type=F32) for p, (_, _, v) in zip(probs, qkvs)]
    return [(o / den, lse) for o, den, lse in zip(outs, dens, lses)]


def _attn_prompt_body(q_ref, k_ref, v_ref, o_ref, lse_ref, *scratch, dil):
    sub_len = SEQ // dil
    n_blocks = sub_len // BAND
    nb_log2 = n_blocks.bit_length() - 1

    assert dil > 1, "the undilated group is computed alongside the sample attention"

    qs_ref, ks_ref, vs_ref, os_ref, ls_ref = scratch
    m = STAGE_CHUNK // dil
    m_log2 = m.bit_length() - 1
    n_chunks = SEQ // STAGE_CHUNK
    row = lax.broadcasted_iota(jnp.int32, (STAGE_CHUNK, STAGE_CHUNK), 0)
    col = lax.broadcasted_iota(jnp.int32, (STAGE_CHUNK, STAGE_CHUNK), 1)
    to_staged = col == ((row & (m - 1)) * dil + (row >> m_log2))
    to_natural = row == ((col & (m - 1)) * dil + (col >> m_log2))
    perm = jnp.where(to_staged, 1.0, 0.0).astype(BF16)
    unperm = jnp.where(to_natural, 1.0, 0.0).astype(BF16)

    def staged_rows(c, r):
        return pl.ds(pl.multiple_of(r * sub_len + c * m, m), m)

    def gather_chunk(ref, c):
        return jnp.concatenate([ref[staged_rows(c, r), :] for r in range(dil)], axis=0)

    def natural_rows(c):
        return pl.ds(pl.multiple_of(c * STAGE_CHUNK, STAGE_CHUNK), STAGE_CHUNK)

    for hp in range(HEADS // STAGE_HEADS):
        pass_lanes = slice(hp * STAGE_HEADS * HEAD_DIM, (hp + 1) * STAGE_HEADS * HEAD_DIM)

        def stage(c, carry, pass_lanes=pass_lanes):
            for src, dst in ((q_ref, qs_ref), (k_ref, ks_ref), (v_ref, vs_ref)):
                y = jnp.dot(perm, src[natural_rows(c), pass_lanes],
                            preferred_element_type=F32).astype(BF16)
                for r in range(dil):
                    dst[staged_rows(c, r), :] = y[r * m:(r + 1) * m, :]
            return carry
        lax.fori_loop(0, n_chunks, stage, 0, unroll=2)

        def tile(t, carry, hp=hp):
            n = t & (n_blocks - 1)
            r0 = (t >> nb_log2) * STAGE_RESIDUES
            if n_blocks == 1:
                kb = 0
                off = 0
            else:
                kb = jnp.maximum(n - 1, 0)
                off = (n - kb) * BAND
            key_rows = min(2 * BAND, sub_len)
            head_lanes = [slice(hh * HEAD_DIM, (hh + 1) * HEAD_DIM) for hh in range(STAGE_HEADS)]
            items, dests = [], []
            for rr in range(STAGE_RESIDUES):
                base = (r0 + rr) * sub_len
                q_rows = pl.ds(pl.multiple_of(base + n * BAND, BAND), BAND)
                k_rows = pl.ds(pl.multiple_of(base + kb * BAND, BAND), key_rows)
                for hh, lanes in enumerate(head_lanes):
                    items.append((qs_ref[q_rows, lanes], ks_ref[k_rows, lanes],
                                  vs_ref[k_rows, lanes]))
                    dests.append((q_rows, lanes, hp * STAGE_HEADS + hh))
            for (q_rows, lanes, h), (o, lse) in zip(dests, _band_softmax(items, off)):
                os_ref[q_rows, lanes] = o.astype(BF16)
                ls_ref[q_rows, h * LSE_REP:(h + 1) * LSE_REP] = jnp.broadcast_to(
                    lse, (BAND, LSE_REP))
            return carry
        lax.fori_loop(0, dil * n_blocks // STAGE_RESIDUES, tile, 0)

        def unstage(c, carry, pass_lanes=pass_lanes):
            o_ref[natural_rows(c), pass_lanes] = jnp.dot(
                unperm, gather_chunk(os_ref, c), preferred_element_type=F32).astype(BF16)
            return carry
        lax.fori_loop(0, n_chunks, unstage, 0, unroll=2)

    def unstage_lse(c, carry):
        z = gather_chunk(ls_ref, c)
        hi = z.astype(BF16)
        rest = z - hi.astype(F32)
        mid = rest.astype(BF16)
        lo = (rest - mid.astype(F32)).astype(BF16)
        move = lambda piece: jnp.dot(unperm, piece, preferred_element_type=F32)
        lse_ref[natural_rows(c), :] = move(hi) + (move(mid) + move(lo))
        return carry
    lax.fori_loop(0, n_chunks, unstage_lse, 0, unroll=2)


def _attn_prompt_group(h3d, g, batch):
    _, dil = DIL_GROUPS[g]
    tile = (None, SEQ, WIDTH)
    in_specs = [
        pl.BlockSpec(tile, lambda b, g=g: (g, b, 0)),
        pl.BlockSpec(tile, lambda b, g=g: (J_K0 + g, b, 0)),
        pl.BlockSpec(tile, lambda b, g=g: (J_V0 + g, b, 0)),
    ]
    scratch = ([pltpu.VMEM((SEQ, STAGE_HEADS * HEAD_DIM), BF16)] * 4
               + [pltpu.VMEM((SEQ, HEADS * LSE_REP), F32)])
    return pl.pallas_call(
        functools.partial(_attn_prompt_body, dil=dil),
        grid=(batch,),
        in_specs=in_specs,
        out_specs=[pl.BlockSpec((SEQ, WIDTH), lambda b: (b, 0)),
                   pl.BlockSpec((SEQ, HEADS * LSE_REP), lambda b: (b, 0))],
        out_shape=[jax.ShapeDtypeStruct((batch * SEQ, WIDTH), BF16),
                   jax.ShapeDtypeStruct((batch * SEQ, HEADS * LSE_REP), F32)],
        scratch_shapes=scratch,
        compiler_params=pltpu.CompilerParams(
            dimension_semantics=("arbitrary",), vmem_limit_bytes=VMEM_LIMIT),
        name="attn_prompt_g%d" % g,
    )(h3d, h3d, h3d)


def _attn_sample_body(q0_ref, q1_ref, q2_ref, kn0_ref, kn1_ref, kn2_ref,
                      vn0_ref, vn1_ref, vn2_ref,
                      ck0_ref, cv0_ref, ck1_ref, cv1_ref, ck2_ref, cv2_ref,
                      pq_ref, pkp_ref, pkc_ref, pvp_ref, pvc_ref, o_ref,
                      nk0_ref, nk1_ref, nk2_ref, nv0_ref, nv1_ref, nv2_ref, po_ref, pl_ref,
                      bc0_ref, bc1_ref, bc2_ref, bn_ref, *, n_new):
    scale = np.float32(HEAD_DIM ** -0.5)
    q_refs = (q0_ref, q1_ref, q2_ref)
    kn_refs = (kn0_ref, kn1_ref, kn2_ref)
    vn_refs = (vn0_ref, vn1_ref, vn2_ref)
    ck_refs = (ck0_ref, ck1_ref, ck2_ref)
    cv_refs = (cv0_ref, cv1_ref, cv2_ref)
    bc_refs = (bc0_ref, bc1_ref, bc2_ref)
    n_q = n_new * HEADS
    heads_log2 = HEADS.bit_length() - 1

    @pl.when(pl.program_id(0) == 0)
    def _():
        for g, (_, dil) in enumerate(DIL_GROUPS):
            n_res = min(dil, n_new)
            dil_log2 = dil.bit_length() - 1
            res_log2 = n_res.bit_length() - 1
            n_keys = BAND * n_res * HEADS
            row = lax.broadcasted_iota(jnp.int32, (n_q, n_keys), 0)
            col = lax.broadcasted_iota(jnp.int32, (n_q, n_keys), 1)
            t = row >> heads_log2
            ok = (((col & (HEADS - 1)) == (row & (HEADS - 1)))
                  & (((col >> heads_log2) & (n_res - 1)) == (t & (dil - 1)))
                  & ((col >> (heads_log2 + res_log2)) >= (t >> dil_log2)))
            bc_refs[g][...] = jnp.where(ok, 0.0, -jnp.inf).astype(F32)
            row = lax.broadcasted_iota(jnp.int32, (n_q, n_q), 0)
            col = lax.broadcasted_iota(jnp.int32, (n_q, n_q), 1)
            dn = (row >> heads_log2) - (col >> heads_log2)
            ok = (((col & (HEADS - 1)) == (row & (HEADS - 1)))
                  & (dn >= 0) & ((dn & (dil - 1)) == 0))
            bn_ref[g] = jnp.where(ok, 0.0, -jnp.inf).astype(F32)

    contract_last = (((1,), (1,)), ((), ()))
    outs, lses = [], []
    for g, (_, dil) in enumerate(DIL_GROUPS):
        n_keys = BAND * min(dil, n_new) * HEADS
        q = q_refs[g][...].reshape(n_q, HEAD_DIM).astype(BF16)
        kn = kn_refs[g][...].reshape(n_q, HEAD_DIM).astype(BF16)
        vn = vn_refs[g][...].reshape(n_q, HEAD_DIM).astype(BF16)
        kc = ck_refs[g][...].reshape(n_keys, HEAD_DIM).astype(BF16)
        vc = cv_refs[g][...].reshape(n_keys, HEAD_DIM).astype(BF16)
        s = lax.dot_general(q, kc, contract_last, preferred_element_type=F32) * scale
        s = s + bc_refs[g][...]
        s_new = lax.dot_general(q, kn, contract_last, preferred_element_type=F32) * scale
        s_new = s_new + bn_ref[g]
        m = jnp.maximum(jnp.max(s, axis=-1, keepdims=True),
                        jnp.max(s_new, axis=-1, keepdims=True))
        p = jnp.exp(s - m)
        p_new = jnp.exp(s_new - m)
        den = jnp.sum(p, axis=-1, keepdims=True) + jnp.sum(p_new, axis=-1, keepdims=True)
        acc = (jnp.dot(p.astype(BF16), vc, preferred_element_type=F32)
               + jnp.dot(p_new.astype(BF16), vn, preferred_element_type=F32))
        outs.append(acc / den)
        lses.append(m + jnp.log(den))
    mx = jnp.maximum(jnp.maximum(lses[0], lses[1]), lses[2])
    ws = [jnp.exp(l - mx) for l in lses]
    tot = ws[0] + ws[1] + ws[2]
    o = (ws[0] * outs[0] + ws[1] * outs[1] + ws[2] * outs[2]) / tot
    o_ref[...] = o.reshape(n_new, HEADS, HEAD_DIM)
    for dst, src in zip((nk0_ref, nk1_ref, nk2_ref, nv0_ref, nv1_ref, nv2_ref),
                        kn_refs + vn_refs):
        dst[...] = src[...]

    first_block = (pl.program_id(0) & (SEQ // BAND - 1)) == 0
    min_key = jnp.where(first_block, BAND, 0)
    head_lanes = [slice(h * HEAD_DIM, (h + 1) * HEAD_DIM) for h in range(HEADS)]
    res = _band_softmax(
        [(pq_ref[:, lanes],
          jnp.concatenate([pkp_ref[:, lanes], pkc_ref[:, lanes]], axis=0),
          jnp.concatenate([pvp_ref[:, lanes], pvc_ref[:, lanes]], axis=0))
         for lanes in head_lanes], BAND, min_key)
    for lanes, (o_blk, _) in zip(head_lanes, res):
        po_ref[:, lanes] = o_blk.astype(BF16)
    pl_ref[...] = _lse_tile([lse for _, lse in res])


def _attn_sample(qkv, caches_k, caches_v, h3d, n_batch, n_new):
    row_block = (n_new, HEADS, HEAD_DIM)
    in_specs = [pl.BlockSpec((None,) + row_block, lambda b, k=k: (k, b, 0, 0))
                for k in range(3 * N_DIL)]
    args = [qkv] * (3 * N_DIL)
    n_q = n_new * HEADS
    scratch = []
    for g, (win, dil) in enumerate(DIL_GROUPS):
        n_res = min(dil, n_new)
        for c in (caches_k[g], caches_v[g]):
            in_specs.append(pl.BlockSpec((None, BAND, n_res, HEADS, HEAD_DIM),
                                         lambda b: (b, 0, 0, 0, 0)))
            args.append(c.reshape(n_batch, win // dil, dil, HEADS, HEAD_DIM))
        scratch.append(pltpu.VMEM((n_q, BAND * n_res * HEADS), F32))
    scratch.append(pltpu.VMEM((N_DIL, n_q, n_q), F32))
    blocks_per_seq = SEQ // BAND
    assert h3d.shape[1] == n_batch * BAND, "one prompt query block per sample step"
    prev = lambda s: jnp.where((s & (blocks_per_seq - 1)) == 0, s, s - 1)
    tile = (None, BAND, WIDTH)
    in_specs += [pl.BlockSpec(tile, lambda s: (0, s, 0)),
                 pl.BlockSpec(tile, lambda s: (J_K0, prev(s), 0)),
                 pl.BlockSpec(tile, lambda s: (J_K0, s, 0)),
                 pl.BlockSpec(tile, lambda s: (J_V0, prev(s), 0)),
                 pl.BlockSpec(tile, lambda s: (J_V0, s, 0))]
    args += [h3d] * 5
    row_spec = pl.BlockSpec(row_block, lambda b: (b, 0, 0))
    n_out = 1 + 2 * N_DIL
    body = functools.partial(_attn_sample_body, n_new=n_new)
    return pl.pallas_call(
        body,
        grid=(n_batch,),
        in_specs=in_specs,
        out_specs=[row_spec] * n_out + [pl.BlockSpec((BAND, WIDTH), lambda s: (s, 0)),
                                        pl.BlockSpec((BAND, HEADS * LSE_REP), lambda s: (s, 0))],
        out_shape=([jax.ShapeDtypeStruct((n_batch * n_new,) + row_block[1:], F32)] * n_out
                   + [jax.ShapeDtypeStruct((h3d.shape[1], WIDTH), BF16),
                      jax.ShapeDtypeStruct((h3d.shape[1], HEADS * LSE_REP), F32)]),
        scratch_shapes=scratch,
        compiler_params=pltpu.CompilerParams(
            dimension_semantics=("arbitrary",), vmem_limit_bytes=VMEM_LIMIT),
        name="attn_sample",
    )(*args)


def _final_body(*refs, tm, mix_rows, alpha, n_groups):
    zb_ref, gu_ref, vn_ref, za_ref, ga0_ref, ga1_ref, gb0_ref, gb1_ref = refs[:8]
    refs = refs[8:]
    if n_groups:
        og_refs = refs[:n_groups]
        lg_refs = refs[n_groups:2 * n_groups]
        refs = refs[2 * n_groups:]
    else:
        o_ref = refs[0]
        refs = refs[1:]
    (x_ref, wmix_ref, bmix_ref, woa_ref, wob_ref, wout_ref, lng_ref, lnb_ref,
     y_ref, ya_ref, yb_ref) = refs

    if n_groups:
        lses = [l_ref[...] for l_ref in lg_refs]
        mx = functools.reduce(jnp.maximum, lses)
        ws = [jnp.exp(l - mx) for l in lses]
        tot = functools.reduce(lambda a, b: a + b, ws)
        ws = [w / tot for w in ws]
        for h in range(HEADS):
            lanes = slice(h * HEAD_DIM, (h + 1) * HEAD_DIM)
            o = None
            for w, og_ref in zip(ws, og_refs):
                term = w[:, h * LSE_REP:h * LSE_REP + 1] * og_ref[:, lanes].astype(F32)
                o = term if o is None else o + term
            yb_ref[:, lanes] = (o * zb_ref[:, lanes].astype(F32)).astype(BF16)
    else:
        yb_ref[...] = (o_ref[...] * zb_ref[...].astype(F32)).astype(BF16)

    row = lax.broadcasted_iota(jnp.int32, (CHUNK, CHUNK), 0)
    col = lax.broadcasted_iota(jnp.int32, (CHUNK, CHUNK), 1)
    mix_log2 = mix_rows.bit_length() - 1
    causal = (row >= col) & ((row >> mix_log2) == (col >> mix_log2))
    for g in range(HEADS):
        lanes = slice(g * CHUNK, (g + 1) * CHUNK)
        w_c = jnp.where(causal, wmix_ref[g], 0.0).astype(BF16)
        for c in range(tm // CHUNK):
            rows = slice(c * CHUNK, (c + 1) * CHUNK)
            mixed = jnp.dot(w_c, vn_ref[rows, lanes], preferred_element_type=F32)
            mixed = mixed + bmix_ref[:, lanes]
            ya = gu_ref[rows, lanes].astype(F32) * mixed * za_ref[rows, lanes].astype(F32)
            ya_ref[rows, lanes] = ya.astype(BF16)

    proj_a = jnp.dot(ya_ref[...], woa_ref[...], preferred_element_type=F32)
    proj_b = jnp.dot(yb_ref[...], wob_ref[...], preferred_element_type=F32)
    ga = jnp.concatenate([ga0_ref[...], ga1_ref[...]], axis=1).astype(F32)
    gb = jnp.concatenate([gb0_ref[...], gb1_ref[...]], axis=1).astype(F32)
    merged = (ga * proj_a + gb * proj_b).astype(BF16)
    z = alpha * x_ref[...] + jnp.dot(merged, wout_ref[...], preferred_element_type=F32)
    y_ref[...] = _layer_norm_rows(z, lng_ref[...], lnb_ref[...])


def _final(h3d, o_parts, lse_parts, x, wmix, bmix, woa, wob, wout, ln_g, ln_b,
           mix_rows, alpha, name):
    n_rows = x.shape[0]
    tm = 256
    n_groups = len(lse_parts)
    hspec = lambda jj: pl.BlockSpec((None, tm, WIDTH), lambda i, jj=jj: (jj, i, 0))
    rowspec = lambda width: pl.BlockSpec((tm, width), lambda i: (i, 0))
    const = lambda shape: pl.BlockSpec(shape, lambda i: (0,) * len(shape),
                                       pipeline_mode=pl.Buffered(1))
    in_specs = [hspec(J_ZB), hspec(J_U), hspec(J_VA), hspec(J_ZA),
                hspec(J_GA), hspec(J_GA + 1), hspec(J_GB), hspec(J_GB + 1)]
    in_specs += [rowspec(WIDTH)] * len(o_parts)
    in_specs += [rowspec(HEADS * LSE_REP)] * n_groups
    in_specs += [rowspec(D_MODEL),
                 const((HEADS, CHUNK, CHUNK)), const((CHUNK, WIDTH)),
                 const((WIDTH, D_MODEL)), const((WIDTH, D_MODEL)), const((D_MODEL, D_MODEL)),
                 const((1, D_MODEL)), const((1, D_MODEL))]
    body = functools.partial(_final_body, tm=tm, mix_rows=mix_rows, alpha=np.float32(alpha),
                             n_groups=n_groups)
    return pl.pallas_call(
        body,
        grid=(n_rows // tm,),
        in_specs=in_specs,
        out_specs=pl.BlockSpec((tm, D_MODEL), lambda i: (i, 0)),
        out_shape=jax.ShapeDtypeStruct((n_rows, D_MODEL), F32),
        scratch_shapes=[pltpu.VMEM((tm, WIDTH), BF16), pltpu.VMEM((tm, WIDTH), BF16)],
        compiler_params=pltpu.CompilerParams(
            dimension_semantics=("arbitrary",), vmem_limit_bytes=VMEM_LIMIT),
        name=name,
    )(*([h3d] * 8), *o_parts, *lse_parts, x, wmix, bmix, woa, wob, wout, ln_g, ln_b)


def _rope_tables(pos):
    half = HEAD_DIM // 2
    inv = ROPE_THETA ** (-jnp.arange(0, half, dtype=F32) * 2.0 / HEAD_DIM)
    ang = pos.astype(F32)[:, None] * inv[None, :]
    cos = jnp.cos(ang)
    sin = jnp.sin(ang)
    return jnp.concatenate([cos, cos], axis=-1), jnp.concatenate([-sin, sin], axis=-1)


def kernel(x_prompt, x_sample, cache_k_w128, cache_v_w128, cache_k_w512, cache_v_w512,
           cache_k_w2048, cache_v_w2048, w_in, b_in, w_s, b_s, ln_v_g, ln_v_b,
           w_o_a, w_o_b, w_out, ln_g, ln_b):
    depth = w_in.shape[0]
    assert depth == 1, "single-layer step"
    batch, seq, _ = x_prompt.shape
    n_dec, n_new, _ = x_sample.shape
    assert seq == SEQ and n_new == 8 and x_prompt.shape[2] == D_MODEL
    caches_k = (cache_k_w128, cache_k_w512, cache_k_w2048)
    caches_v = (cache_v_w128, cache_v_w512, cache_v_w2048)
    for g, (win, _) in enumerate(DIL_GROUPS):
        assert caches_k[g].shape == (depth, n_dec, win, HEADS, HEAD_DIM)
    past_len = cache_k_w2048.shape[2]
    alpha = float(2 * depth) ** 0.25

    w_f32 = w_in.reshape(D_MODEL, -1)
    b2 = b_in.reshape(1, -1)
    lvg = ln_v_g.reshape(1, WIDTH)
    lvb = ln_v_b.reshape(1, WIDTH)
    lg = ln_g.reshape(1, D_MODEL)
    lb = ln_b.reshape(1, D_MODEL)
    ws = w_s.reshape(HEADS, CHUNK, CHUNK)
    bs = b_s.reshape(HEADS, CHUNK)
    woa = w_o_a.reshape(WIDTH, D_MODEL).astype(BF16)
    wob = w_o_b.reshape(WIDTH, D_MODEL).astype(BF16)
    wout = w_out.reshape(D_MODEL, D_MODEL).astype(BF16)

    cos_p, sin_p = _rope_tables(jnp.arange(seq, dtype=jnp.int32))
    pos_s = past_len + jnp.tile(jnp.arange(n_new, dtype=jnp.int32), n_dec)
    cos_s, sin_s = _rope_tables(pos_s)

    xp2 = x_prompt.reshape(batch * seq, D_MODEL)
    xs2 = x_sample.reshape(n_dec * n_new, D_MODEL)

    pouts = _inproj_prompt(xp2.astype(BF16), w_f32, b2, cos_p, sin_p, lvg, lvb)
    h3d = pouts[0]
    new_pk, new_pv = pouts[1:4], pouts[4:7]
    hs3d, qkv_s, gv_s = _inproj_sample(xs2.astype(BF16), w_f32, b2, cos_s, sin_s, lvg, lvb)
    qkv_s = qkv_s.reshape(3 * N_DIL, n_dec * n_new, HEADS, HEAD_DIM)

    aouts = _attn_sample(qkv_s, caches_k, caches_v, h3d, n_dec, n_new)
    o_s, kn_s, vn_s = aouts[0], aouts[1:1 + N_DIL], aouts[1 + N_DIL:1 + 2 * N_DIL]
    o_parts, lse_parts = [aouts[-2]], [aouts[-1]]
    for g in range(1, N_DIL):
        o_g, lse_g = _attn_prompt_group(h3d, g, batch)
        o_parts.append(o_g)
        lse_parts.append(lse_g)

    bmix_p = jnp.repeat(bs.T, CHUNK, axis=1)
    yp = _final(h3d, o_parts, lse_parts, xp2, ws, bmix_p, woa, wob, wout, lg, lb,
                CHUNK, alpha, "final_prompt")
    reps = CHUNK // n_new
    wmix_s = jnp.tile(ws[:, :n_new, :n_new], (1, reps, reps))
    bmix_s = jnp.repeat(jnp.tile(bs[:, :n_new], (1, reps)).T, CHUNK, axis=1)
    ys = _final(hs3d, [o_s.reshape(n_dec * n_new, WIDTH)], [], xs2, wmix_s, bmix_s,
                woa, wob, wout, lg, lb, n_new, alpha, "final_sample")

    shp_p = lambda a: a.reshape(depth, batch, -1, HEADS, HEAD_DIM)
    new_p = tuple(shp_p(a) for pair in zip(new_pk, new_pv) for a in pair)
    shp_s = lambda a: a.reshape(depth, n_dec, n_new, HEADS, HEAD_DIM)
    new_s = tuple(shp_s(a) for pair in zip(kn_s, vn_s) for a in pair)
    return ((yp.reshape(batch, seq, D_MODEL), ys.reshape(n_dec, n_new, D_MODEL))
            + new_p + new_s + (gv_s.reshape(depth, n_dec, n_new, WIDTH),))
```

```python
import functools

import numpy as np
import jax
import jax.numpy as jnp
from jax import lax
from jax.experimental import pallas as pl
from jax.experimental.pallas import tpu as pltpu

F32 = jnp.float32
BF16 = jnp.bfloat16

D_MODEL = 2048
HEAD_DIM = 128
HEADS = 8
DIL_GROUPS = ((128, 1), (512, 4), (2048, 16))
N_DIL = len(DIL_GROUPS)
BAND = 128
WIDTH = HEADS * HEAD_DIM
CHUNK = 128
SEQ = 2048
ROPE_THETA = 10000.0
LN_EPS = 1e-5
N_COL_TILES = 17
J_K0, J_V0, J_ZB, J_U, J_VA, J_ZA, J_GA, J_GB = 3, 6, 9, 10, 11, 12, 13, 15
ALL_COLS = tuple(range(N_COL_TILES))
KV_COLS = tuple(range(J_K0, J_ZB))
OTHER_COLS = tuple(c for c in ALL_COLS if c not in KV_COLS)
EPILOGUE_ROWS = 64
MXU_COLS = 256
STAGE_CHUNK = 256
STAGE_HEADS = 4
STAGE_RESIDUES = 2
LSE_REP_LOG2 = 4
LSE_REP = 1 << LSE_REP_LOG2
VMEM_LIMIT = 56 * 1024 * 1024


def _gelu(x):
    return 0.5 * x * (1.0 + lax.erf(x * np.float32(np.sqrt(0.5))))


def _sigmoid(x):
    return 0.5 * (jnp.tanh(0.5 * x) + 1.0)


def _silu(x):
    return x * _sigmoid(x)


def _layer_norm_rows(x, g, b):
    mu = jnp.mean(x, axis=-1, keepdims=True)
    xc = x - mu
    var = jnp.mean(xc * xc, axis=-1, keepdims=True)
    return xc * lax.rsqrt(var + LN_EPS) * g + b


def _col_of(cols, jj):
    cols = list(cols)
    gaps = [k for k in range(1, len(cols)) if cols[k] != cols[k - 1] + 1]
    assert len(gaps) <= 1
    if not gaps:
        return jj + cols[0]
    k = gaps[0]
    return jnp.where(jj < k, jj + cols[0], jj + (cols[k] - k))


def _inproj_body(x_ref, w_ref, b_ref, cos_ref, sin_ref, lng_ref, lnb_ref, *rest,
                 tm, f32_specs, cols):
    h_ref = rest[0]
    f32_refs = rest[1:1 + len(f32_specs)]
    acc_ref, wbf_ref = rest[-2:]
    j = _col_of(cols, pl.program_id(0))
    i = pl.program_id(1)
    has = lambda lo, hi: any(lo <= c < hi for c in cols)
    tiles_per_batch = SEQ // tm
    last_of_batch = i % tiles_per_batch == tiles_per_batch - 1

    @pl.when(i == 0)
    def _():
        def narrow(c, carry):
            rows = pl.ds(pl.multiple_of(c * MXU_COLS, MXU_COLS), MXU_COLS)
            wbf_ref[rows, :] = w_ref[rows, :].astype(BF16)
            return carry
        lax.fori_loop(0, D_MODEL // MXU_COLS, narrow, 0)

    def project(fn, emit_bf16=True):
        for c in range(WIDTH // MXU_COLS):
            cols = slice(c * MXU_COLS, (c + 1) * MXU_COLS)
            a = (jnp.dot(x_ref[...], wbf_ref[:, cols], preferred_element_type=F32)
                 + b_ref[:, cols])
            res = fn(a)
            acc_ref[:, cols] = res
            if emit_bf16:
                h_ref[:, cols] = res.astype(BF16)

    def rope(a):
        cos = cos_ref[...]
        sin = sin_ref[...]
        parts = []
        for h in range(MXU_COLS // HEAD_DIM):
            ah = a[:, h * HEAD_DIM:(h + 1) * HEAD_DIM]
            parts.append(ah * cos + pltpu.roll(ah, HEAD_DIM // 2, 1) * sin)
        return jnp.concatenate(parts, axis=1)

    if has(0, J_V0):
        @pl.when(j < J_V0)
        def _():
            project(rope)

    if has(J_V0, J_ZB):
        @pl.when((j >= J_V0) & (j < J_ZB))
        def _():
            project(lambda a: a)

    if has(J_ZB, J_ZB + 1) or has(J_ZA, J_ZA + 1):
        @pl.when((j == J_ZB) | (j == J_ZA))
        def _():
            project(_silu)

    if has(J_U, J_U + 1):
        @pl.when(j == J_U)
        def _():
            project(_gelu)

    if has(J_VA, J_VA + 1):
        @pl.when(j == J_VA)
        def _():
            project(_gelu, emit_bf16=False)

            def normalise(c, carry):
                rows = pl.ds(pl.multiple_of(c * EPILOGUE_ROWS, EPILOGUE_ROWS), EPILOGUE_ROWS)
                res = _layer_norm_rows(acc_ref[rows, :], lng_ref[...], lnb_ref[...])
                acc_ref[rows, :] = res
                h_ref[rows, :] = res.astype(BF16)
                return carry
            lax.fori_loop(0, tm // EPILOGUE_ROWS, normalise, 0)

    if has(J_GA, N_COL_TILES):
        @pl.when(j >= J_GA)
        def _():
            project(_sigmoid)

    def scatter_heads(o_ref, row0, n_rows):
        for h in range(HEADS):
            o_ref[pl.ds(h, n_rows, stride=HEADS), :] = (
                acc_ref[row0:row0 + n_rows, h * HEAD_DIM:(h + 1) * HEAD_DIM])

    for (j_own, kind), o_ref in zip(f32_specs, f32_refs):
        if kind == 'heads':
            @pl.when(j == j_own)
            def _(o_ref=o_ref):
                scatter_heads(o_ref, 0, tm)
        elif kind == 'heads_below':
            @pl.when(j < j_own)
            def _(o_ref=o_ref):
                scatter_heads(o_ref, 0, tm)
        elif kind == 'heads_last':
            @pl.when((j == j_own) & last_of_batch)
            def _(o_ref=o_ref):
                scatter_heads(o_ref, 0, tm)
        elif kind == 'heads_tail':
            @pl.when((j == j_own) & last_of_batch)
            def _(o_ref=o_ref):
                scatter_heads(o_ref, tm - 128, 128)
        elif kind == 'flat':
            @pl.when(j == j_own)
            def _(o_ref=o_ref):
                o_ref[...] = acc_ref[...]
        else:
            raise ValueError(kind)


def _sticky(j_own, first, last, idx_fn, cols=ALL_COLS):
    def index_map(jj, i):
        j = _col_of(cols, jj)
        idx = idx_fn(i)
        return tuple(jnp.where(j < j_own, f, jnp.where(j > j_own, l, k))
                     for f, l, k in zip(first, last, idx))
    return index_map


def _inproj_call(x_bf, w_f32, b_in, cos_tab, sin_tab, ln_g, ln_b, *, tm, table_tiles,
                 f32_specs, f32_shapes, f32_blocks, name, cols=ALL_COLS):
    n_rows = x_bf.shape[0]
    n_i = n_rows // tm
    col = lambda jj: _col_of(cols, jj)
    in_specs = [
        pl.BlockSpec((tm, D_MODEL), lambda jj, i: (i, 0)),
        pl.BlockSpec((D_MODEL, WIDTH), lambda jj, i: (0, col(jj))),
        pl.BlockSpec((1, WIDTH), lambda jj, i: (0, col(jj))),
        pl.BlockSpec((tm, HEAD_DIM), lambda jj, i: (i % table_tiles, 0)),
        pl.BlockSpec((tm, HEAD_DIM), lambda jj, i: (i % table_tiles, 0)),
        pl.BlockSpec((1, WIDTH), lambda jj, i: (0, 0)),
        pl.BlockSpec((1, WIDTH), lambda jj, i: (0, 0)),
    ]
    out_shapes = [jax.ShapeDtypeStruct((len(cols), n_rows, WIDTH), BF16)] + list(f32_shapes)
    out_specs = ([pl.BlockSpec((None, tm, WIDTH), lambda jj, i: (jj, i, 0))]
                 + list(f32_blocks))
    body = functools.partial(_inproj_body, tm=tm, f32_specs=tuple(f32_specs),
                             cols=tuple(cols))
    return pl.pallas_call(
        body,
        grid=(len(cols), n_i),
        in_specs=in_specs,
        out_specs=out_specs,
        out_shape=out_shapes,
        scratch_shapes=[pltpu.VMEM((tm, WIDTH), F32), pltpu.VMEM((D_MODEL, WIDTH), BF16)],
        compiler_params=pltpu.CompilerParams(
            dimension_semantics=("arbitrary", "arbitrary"), vmem_limit_bytes=VMEM_LIMIT),
        name=name,
    )(x_bf, w_f32, b_in, cos_tab, sin_tab, ln_g, ln_b)


def _inproj_prompt(x_bf, w_f32, b_in, cos_tab, sin_tab, ln_g, ln_b):
    batch = x_bf.shape[0] // SEQ
    kv_cols = KV_COLS
    tm = 512
    tpb = SEQ // tm
    specs, shapes, blocks = [], [], []
    for base in (J_K0, J_V0):
        for g, (win, _) in enumerate(DIL_GROUPS):
            keep = min(win, SEQ)
            shapes.append(jax.ShapeDtypeStruct((batch * keep * HEADS, HEAD_DIM), F32))
            if keep < tm:
                specs.append((base + g, 'heads_tail'))
                idx_fn = lambda i: (i // tpb, 0)
                rows = keep
            elif keep == tm:
                specs.append((base + g, 'heads_last'))
                idx_fn = lambda i: (i // tpb, 0)
                rows = tm
            else:
                assert keep == SEQ
                specs.append((base + g, 'heads'))
                idx_fn = lambda i: (i, 0)
                rows = tm
            last = (batch * keep // rows - 1, 0)
            blocks.append(pl.BlockSpec((rows * HEADS, HEAD_DIM),
                                       _sticky(base + g, (0, 0), last, idx_fn, kv_cols)))
    outs = _inproj_call(x_bf, w_f32, b_in, cos_tab, sin_tab, ln_g, ln_b, tm=tm,
                        table_tiles=tpb, f32_specs=specs, f32_shapes=shapes,
                        f32_blocks=blocks, name="inproj_prompt_kv", cols=kv_cols)
    tm_other = 1024
    h_other = _inproj_call(x_bf, w_f32, b_in, cos_tab, sin_tab, ln_g, ln_b, tm=tm_other,
                           table_tiles=SEQ // tm_other, f32_specs=[], f32_shapes=[],
                           f32_blocks=[], name="inproj_prompt_other", cols=OTHER_COLS)[0]
    return [outs[0], h_other] + list(outs[1:])


def _inproj_sample(x_bf, w_f32, b_in, cos_tab, sin_tab, ln_g, ln_b):
    n_rows = x_bf.shape[0]
    tm = 512
    n_i = n_rows // tm
    specs = [(J_ZB, 'heads_below')]
    shapes = [jax.ShapeDtypeStruct((J_ZB, n_rows * HEADS, HEAD_DIM), F32)]
    blocks = [pl.BlockSpec(
        (None, tm * HEADS, HEAD_DIM),
        lambda j, i: (jnp.minimum(j, J_ZB - 1), jnp.where(j < J_ZB, i, n_i - 1), 0))]
    specs.append((J_VA, 'flat'))
    shapes.append(jax.ShapeDtypeStruct((n_rows, WIDTH), F32))
    blocks.append(pl.BlockSpec((tm, WIDTH),
                               _sticky(J_VA, (0, 0), (n_i - 1, 0), lambda i: (i, 0))))
    return _inproj_call(x_bf, w_f32, b_in, cos_tab, sin_tab, ln_g, ln_b, tm=tm,
                        table_tiles=n_i, f32_specs=specs, f32_shapes=shapes,
                        f32_blocks=blocks, name="inproj_sample")


def _lse_tile(cols):
    rows = cols[0].shape[0]
    lane_head = lax.broadcasted_iota(jnp.int32, (rows, HEADS * LSE_REP), 1) >> LSE_REP_LOG2
    tile = jnp.zeros((rows, HEADS * LSE_REP), F32)
    for h, c in enumerate(cols):
        tile = jnp.where(lane_head == h, c, tile)
    return tile


def _band_softmax(qkvs, off, min_key=None):
    win = qkvs[0][1].shape[0]
    scale = np.float32(HEAD_DIM ** -0.5)
    qi = lax.broadcasted_iota(jnp.int32, (BAND, win), 0)
    ki = lax.broadcasted_iota(jnp.int32, (BAND, win), 1)
    diff = qi - ki + off
    mask = (diff >= 0) & (diff <= BAND)
    if min_key is not None:
        mask = mask & (ki >= min_key)
    scores = [lax.dot_general(q, k, (((1,), (1,)), ((), ())), preferred_element_type=F32)
              for q, k, _ in qkvs]
    probs, dens, lses = [], [], []
    for s in scores:
        s = jnp.where(mask, s * scale, -jnp.inf)
        m = jnp.max(s, axis=-1, keepdims=True)
        p = jnp.exp(s - m)
        den = jnp.sum(p, axis=-1, keepdims=True)
        probs.append(p.astype(BF16))
        dens.append(den)
        lses.append(m + jnp.log(den))
    outs = [jnp.dot(p, v, preferred_element_type=F32) for p, (_, _, v) in zip(probs, qkvs)]
    return [(o / den, lse) for o, den, lse in zip(outs, dens, lses)]


def _attn_prompt_body(q_ref, k_ref, v_ref, o_ref, lse_ref, *scratch, dil):
    sub_len = SEQ // dil
    n_blocks = sub_len // BAND
    nb_log2 = n_blocks.bit_length() - 1

    assert dil > 1, "the undilated group is computed alongside the sample attention"

    qs_ref, ks_ref, vs_ref, os_ref, ls_ref = scratch
    m = STAGE_CHUNK // dil
    m_log2 = m.bit_length() - 1
    n_chunks = SEQ // STAGE_CHUNK
    row = lax.broadcasted_iota(jnp.int32, (STAGE_CHUNK, STAGE_CHUNK), 0)
    col = lax.broadcasted_iota(jnp.int32, (STAGE_CHUNK, STAGE_CHUNK), 1)
    to_staged = col == ((row & (m - 1)) * dil + (row >> m_log2))
    to_natural = row == ((col & (m - 1)) * dil + (col >> m_log2))
    perm = jnp.where(to_staged, 1.0, 0.0).astype(BF16)
    unperm = jnp.where(to_natural, 1.0, 0.0).astype(BF16)

    def staged_rows(c, r):
        return pl.ds(pl.multiple_of(r * sub_len + c * m, m), m)

    def gather_chunk(ref, c):
        return jnp.concatenate([ref[staged_rows(c, r), :] for r in range(dil)], axis=0)

    def natural_rows(c):
        return pl.ds(pl.multiple_of(c * STAGE_CHUNK, STAGE_CHUNK), STAGE_CHUNK)

    for hp in range(HEADS // STAGE_HEADS):
        pass_lanes = slice(hp * STAGE_HEADS * HEAD_DIM, (hp + 1) * STAGE_HEADS * HEAD_DIM)

        def stage(c, carry, pass_lanes=pass_lanes):
            for src, dst in ((q_ref, qs_ref), (k_ref, ks_ref), (v_ref, vs_ref)):
                y = jnp.dot(perm, src[natural_rows(c), pass_lanes],
                            preferred_element_type=F32).astype(BF16)
                for r in range(dil):
                    dst[staged_rows(c, r), :] = y[r * m:(r + 1) * m, :]
            return carry
        lax.fori_loop(0, n_chunks, stage, 0, unroll=2)

        def tile(t, carry, hp=hp):
            n = t & (n_blocks - 1)
            r0 = (t >> nb_log2) * STAGE_RESIDUES
            if n_blocks == 1:
                kb = 0
                off = 0
            else:
                kb = jnp.maximum(n - 1, 0)
                off = (n - kb) * BAND
            key_rows = min(2 * BAND, sub_len)
            head_lanes = [slice(hh * HEAD_DIM, (hh + 1) * HEAD_DIM) for hh in range(STAGE_HEADS)]
            items, dests = [], []
            for rr in range(STAGE_RESIDUES):
                base = (r0 + rr) * sub_len
                q_rows = pl.ds(pl.multiple_of(base + n * BAND, BAND), BAND)
                k_rows = pl.ds(pl.multiple_of(base + kb * BAND, BAND), key_rows)
                for hh, lanes in enumerate(head_lanes):
                    items.append((qs_ref[q_rows, lanes], ks_ref[k_rows, lanes],
                                  vs_ref[k_rows, lanes]))
                    dests.append((q_rows, lanes, hp * STAGE_HEADS + hh))
            for (q_rows, lanes, h), (o, lse) in zip(dests, _band_softmax(items, off)):
                os_ref[q_rows, lanes] = o.astype(BF16)
                ls_ref[q_rows, h * LSE_REP:(h + 1) * LSE_REP] = jnp.broadcast_to(
                    lse, (BAND, LSE_REP))
            return carry
        lax.fori_loop(0, dil * n_blocks // STAGE_RESIDUES, tile, 0)

        def unstage(c, carry, pass_lanes=pass_lanes):
            o_ref[natural_rows(c), pass_lanes] = jnp.dot(
                unperm, gather_chunk(os_ref, c), preferred_element_type=F32).astype(BF16)
            return carry
        lax.fori_loop(0, n_chunks, unstage, 0, unroll=2)

    def unstage_lse(c, carry):
        z = gather_chunk(ls_ref, c)
        hi = z.astype(BF16)
        rest = z - hi.astype(F32)
        mid = rest.astype(BF16)
        lo = (rest - mid.astype(F32)).astype(BF16)
        move = lambda piece: jnp.dot(unperm, piece, preferred_element_type=F32)
        lse_ref[natural_rows(c), :] = move(hi) + (move(mid) + move(lo))
        return carry
    lax.fori_loop(0, n_chunks, unstage_lse, 0, unroll=2)


def _attn_prompt_group(h_kv, h_other, g, batch):
    _, dil = DIL_GROUPS[g]
    tile = (None, SEQ, WIDTH)
    in_specs = [
        pl.BlockSpec(tile, lambda b, g=g: (OTHER_COLS.index(g), b, 0)),
        pl.BlockSpec(tile, lambda b, g=g: (KV_COLS.index(J_K0 + g), b, 0)),
        pl.BlockSpec(tile, lambda b, g=g: (KV_COLS.index(J_V0 + g), b, 0)),
    ]
    scratch = ([pltpu.VMEM((SEQ, STAGE_HEADS * HEAD_DIM), BF16)] * 4
               + [pltpu.VMEM((SEQ, HEADS * LSE_REP), F32)])
    return pl.pallas_call(
        functools.partial(_attn_prompt_body, dil=dil),
        grid=(batch,),
        in_specs=in_specs,
        out_specs=[pl.BlockSpec((SEQ, WIDTH), lambda b: (b, 0)),
                   pl.BlockSpec((SEQ, HEADS * LSE_REP), lambda b: (b, 0))],
        out_shape=[jax.ShapeDtypeStruct((batch * SEQ, WIDTH), BF16),
                   jax.ShapeDtypeStruct((batch * SEQ, HEADS * LSE_REP), F32)],
        scratch_shapes=scratch,
        compiler_params=pltpu.CompilerParams(
            dimension_semantics=("arbitrary",), vmem_limit_bytes=VMEM_LIMIT),
        name="attn_prompt_g%d" % g,
    )(h_other, h_kv, h_kv)


def _attn_sample_body(q0_ref, q1_ref, q2_ref, kn0_ref, kn1_ref, kn2_ref,
                      vn0_ref, vn1_ref, vn2_ref,
                      ck0_ref, cv0_ref, ck1_ref, cv1_ref, ck2_ref, cv2_ref,
                      pq_ref, pkp_ref, pkc_ref, pvp_ref, pvc_ref, o_ref,
                      nk0_ref, nk1_ref, nk2_ref, nv0_ref, nv1_ref, nv2_ref, po_ref, pl_ref,
                      bc0_ref, bc1_ref, bc2_ref, bn_ref, *, n_new):
    scale = np.float32(HEAD_DIM ** -0.5)
    q_refs = (q0_ref, q1_ref, q2_ref)
    kn_refs = (kn0_ref, kn1_ref, kn2_ref)
    vn_refs = (vn0_ref, vn1_ref, vn2_ref)
    ck_refs = (ck0_ref, ck1_ref, ck2_ref)
    cv_refs = (cv0_ref, cv1_ref, cv2_ref)
    bc_refs = (bc0_ref, bc1_ref, bc2_ref)
    n_q = n_new * HEADS
    heads_log2 = HEADS.bit_length() - 1

    @pl.when(pl.program_id(0) == 0)
    def _():
        for g, (_, dil) in enumerate(DIL_GROUPS):
            n_res = min(dil, n_new)
            dil_log2 = dil.bit_length() - 1
            res_log2 = n_res.bit_length() - 1
            n_keys = BAND * n_res * HEADS
            row = lax.broadcasted_iota(jnp.int32, (n_q, n_keys), 0)
            col = lax.broadcasted_iota(jnp.int32, (n_q, n_keys), 1)
            t = row >> heads_log2
            ok = (((col & (HEADS - 1)) == (row & (HEADS - 1)))
                  & (((col >> heads_log2) & (n_res - 1)) == (t & (dil - 1)))
                  & ((col >> (heads_log2 + res_log2)) >= (t >> dil_log2)))
            bc_refs[g][...] = jnp.where(ok, 0.0, -jnp.inf).astype(F32)
            row = lax.broadcasted_iota(jnp.int32, (n_q, n_q), 0)
            col = lax.broadcasted_iota(jnp.int32, (n_q, n_q), 1)
            dn = (row >> heads_log2) - (col >> heads_log2)
            ok = (((col & (HEADS - 1)) == (row & (HEADS - 1)))
                  & (dn >= 0) & ((dn & (dil - 1)) == 0))
            bn_ref[g] = jnp.where(ok, 0.0, -jnp.inf).astype(F32)

    contract_last = (((1,), (1,)), ((), ()))
    outs, lses = [], []
    for g, (_, dil) in enumerate(DIL_GROUPS):
        n_keys = BAND * min(dil, n_new) * HEADS
        q = q_refs[g][...].reshape(n_q, HEAD_DIM).astype(BF16)
        kn = kn_refs[g][...].reshape(n_q, HEAD_DIM).astype(BF16)
        vn = vn_refs[g][...].reshape(n_q, HEAD_DIM).astype(BF16)
        kc = ck_refs[g][...].reshape(n_keys, HEAD_DIM).astype(BF16)
        vc = cv_refs[g][...].reshape(n_keys, HEAD_DIM).astype(BF16)
        s = lax.dot_general(q, kc, contract_last, preferred_element_type=F32) * scale
        s = s + bc_refs[g][...]
        s_new = lax.dot_general(q, kn, contract_last, preferred_element_type=F32) * scale
        s_new = s_new + bn_ref[g]
        m = jnp.maximum(jnp.max(s, axis=-1, keepdims=True),
                        jnp.max(s_new, axis=-1, keepdims=True))
        p = jnp.exp(s - m)
        p_new = jnp.exp(s_new - m)
        den = jnp.sum(p, axis=-1, keepdims=True) + jnp.sum(p_new, axis=-1, keepdims=True)
        acc = (jnp.dot(p.astype(BF16), vc, preferred_element_type=F32)
               + jnp.dot(p_new.astype(BF16), vn, preferred_element_type=F32))
        outs.append(acc / den)
        lses.append(m + jnp.log(den))
    mx = jnp.maximum(jnp.maximum(lses[0], lses[1]), lses[2])
    ws = [jnp.exp(l - mx) for l in lses]
    tot = ws[0] + ws[1] + ws[2]
    o = (ws[0] * outs[0] + ws[1] * outs[1] + ws[2] * outs[2]) / tot
    o_ref[...] = o.reshape(n_new, HEADS, HEAD_DIM)
    for dst, src in zip((nk0_ref, nk1_ref, nk2_ref, nv0_ref, nv1_ref, nv2_ref),
                        kn_refs + vn_refs):
        dst[...] = src[...]

    first_block = (pl.program_id(0) & (SEQ // BAND - 1)) == 0
    min_key = jnp.where(first_block, BAND, 0)
    head_lanes = [slice(h * HEAD_DIM, (h + 1) * HEAD_DIM) for h in range(HEADS)]
    res = _band_softmax(
        [(pq_ref[:, lanes],
          jnp.concatenate([pkp_ref[:, lanes], pkc_ref[:, lanes]], axis=0),
          jnp.concatenate([pvp_ref[:, lanes], pvc_ref[:, lanes]], axis=0))
         for lanes in head_lanes], BAND, min_key)
    for lanes, (o_blk, _) in zip(head_lanes, res):
        po_ref[:, lanes] = o_blk.astype(BF16)
    pl_ref[...] = _lse_tile([lse for _, lse in res])


def _attn_sample(qkv, caches_k, caches_v, h_kv, h_other, n_batch, n_new):
    row_block = (n_new, HEADS, HEAD_DIM)
    in_specs = [pl.BlockSpec((None,) + row_block, lambda b, k=k: (k, b, 0, 0))
                for k in range(3 * N_DIL)]
    args = [qkv] * (3 * N_DIL)
    n_q = n_new * HEADS
    scratch = []
    for g, (win, dil) in enumerate(DIL_GROUPS):
        n_res = min(dil, n_new)
        for c in (caches_k[g], caches_v[g]):
            in_specs.append(pl.BlockSpec((None, BAND, n_res, HEADS, HEAD_DIM),
                                         lambda b: (b, 0, 0, 0, 0)))
            args.append(c.reshape(n_batch, win // dil, dil, HEADS, HEAD_DIM))
        scratch.append(pltpu.VMEM((n_q, BAND * n_res * HEADS), F32))
    scratch.append(pltpu.VMEM((N_DIL, n_q, n_q), F32))
    blocks_per_seq = SEQ // BAND
    n_prompt_rows = h_kv.shape[1]
    assert n_prompt_rows == n_batch * BAND, "one prompt query block per sample step"
    k_idx, v_idx = KV_COLS.index(J_K0), KV_COLS.index(J_V0)
    prev = lambda s: jnp.where((s & (blocks_per_seq - 1)) == 0, s, s - 1)
    tile = (None, BAND, WIDTH)
    in_specs += [pl.BlockSpec(tile, lambda s: (OTHER_COLS.index(0), s, 0)),
                 pl.BlockSpec(tile, lambda s: (k_idx, prev(s), 0)),
                 pl.BlockSpec(tile, lambda s: (k_idx, s, 0)),
                 pl.BlockSpec(tile, lambda s: (v_idx, prev(s), 0)),
                 pl.BlockSpec(tile, lambda s: (v_idx, s, 0))]
    args += [h_other] + [h_kv] * 4
    row_spec = pl.BlockSpec(row_block, lambda b: (b, 0, 0))
    n_out = 1 + 2 * N_DIL
    body = functools.partial(_attn_sample_body, n_new=n_new)
    return pl.pallas_call(
        body,
        grid=(n_batch,),
        in_specs=in_specs,
        out_specs=[row_spec] * n_out + [pl.BlockSpec((BAND, WIDTH), lambda s: (s, 0)),
                                        pl.BlockSpec((BAND, HEADS * LSE_REP), lambda s: (s, 0))],
        out_shape=([jax.ShapeDtypeStruct((n_batch * n_new,) + row_block[1:], F32)] * n_out
                   + [jax.ShapeDtypeStruct((n_prompt_rows, WIDTH), BF16),
                      jax.ShapeDtypeStruct((n_prompt_rows, HEADS * LSE_REP), F32)]),
        scratch_shapes=scratch,
        compiler_params=pltpu.CompilerParams(
            dimension_semantics=("arbitrary",), vmem_limit_bytes=VMEM_LIMIT),
        name="attn_sample",
    )(*args)


def _final_body(*refs, tm, mix_rows, alpha, n_groups):
    zb_ref, gu_ref, vn_ref, za_ref, ga0_ref, ga1_ref, gb0_ref, gb1_ref = refs[:8]
    refs = refs[8:]
    if n_groups:
        og_refs = refs[:n_groups]
        lg_refs = refs[n_groups:2 * n_groups]
        refs = refs[2 * n_groups:]
    else:
        o_ref = refs[0]
        refs = refs[1:]
    (x_ref, wmix_ref, bmix_ref, woa_ref, wob_ref, wout_ref, lng_ref, lnb_ref,
     y_ref, ya_ref, yb_ref) = refs

    if n_groups:
        lses = [l_ref[...] for l_ref in lg_refs]
        mx = functools.reduce(jnp.maximum, lses)
        ws = [jnp.exp(l - mx) for l in lses]
        tot = functools.reduce(lambda a, b: a + b, ws)
        ws = [w / tot for w in ws]
        for h in range(HEADS):
            lanes = slice(h * HEAD_DIM, (h + 1) * HEAD_DIM)
            o = None
            for w, og_ref in zip(ws, og_refs):
                term = w[:, h * LSE_REP:h * LSE_REP + 1] * og_ref[:, lanes].astype(F32)
                o = term if o is None else o + term
            yb_ref[:, lanes] = (o * zb_ref[:, lanes].astype(F32)).astype(BF16)
    else:
        yb_ref[...] = (o_ref[...] * zb_ref[...].astype(F32)).astype(BF16)

    row = lax.broadcasted_iota(jnp.int32, (CHUNK, CHUNK), 0)
    col = lax.broadcasted_iota(jnp.int32, (CHUNK, CHUNK), 1)
    mix_log2 = mix_rows.bit_length() - 1
    causal = (row >= col) & ((row >> mix_log2) == (col >> mix_log2))
    for g in range(HEADS):
        lanes = slice(g * CHUNK, (g + 1) * CHUNK)
        w_c = jnp.where(causal, wmix_ref[g], 0.0).astype(BF16)
        for c in range(tm // CHUNK):
            rows = slice(c * CHUNK, (c + 1) * CHUNK)
            mixed = jnp.dot(w_c, vn_ref[rows, lanes], preferred_element_type=F32)
            mixed = mixed + bmix_ref[:, lanes]
            ya = gu_ref[rows, lanes].astype(F32) * mixed * za_ref[rows, lanes].astype(F32)
            ya_ref[rows, lanes] = ya.astype(BF16)

    proj_a = jnp.dot(ya_ref[...], woa_ref[...], preferred_element_type=F32)
    proj_b = jnp.dot(yb_ref[...], wob_ref[...], preferred_element_type=F32)
    ga = jnp.concatenate([ga0_ref[...], ga1_ref[...]], axis=1).astype(F32)
    gb = jnp.concatenate([gb0_ref[...], gb1_ref[...]], axis=1).astype(F32)
    merged = (ga * proj_a + gb * proj_b).astype(BF16)
    z = alpha * x_ref[...] + jnp.dot(merged, wout_ref[...], preferred_element_type=F32)
    y_ref[...] = _layer_norm_rows(z, lng_ref[...], lnb_ref[...])


def _final(h, h_cols, o_parts, lse_parts, x, wmix, bmix, woa, wob, wout, ln_g, ln_b,
           mix_rows, alpha, name):
    n_rows = x.shape[0]
    tm = 256
    n_groups = len(lse_parts)
    hspec = lambda j: pl.BlockSpec((None, tm, WIDTH),
                                   lambda i, k=h_cols.index(j): (k, i, 0))
    rowspec = lambda width: pl.BlockSpec((tm, width), lambda i: (i, 0))
    const = lambda shape: pl.BlockSpec(shape, lambda i: (0,) * len(shape),
                                       pipeline_mode=pl.Buffered(1))
    in_specs = [hspec(J_ZB), hspec(J_U), hspec(J_VA), hspec(J_ZA),
                hspec(J_GA), hspec(J_GA + 1), hspec(J_GB), hspec(J_GB + 1)]
    in_specs += [rowspec(WIDTH)] * len(o_parts)
    in_specs += [rowspec(HEADS * LSE_REP)] * n_groups
    in_specs += [rowspec(D_MODEL),
                 const((HEADS, CHUNK, CHUNK)), const((CHUNK, WIDTH)),
                 const((WIDTH, D_MODEL)), const((WIDTH, D_MODEL)), const((D_MODEL, D_MODEL)),
                 const((1, D_MODEL)), const((1, D_MODEL))]
    body = functools.partial(_final_body, tm=tm, mix_rows=mix_rows, alpha=np.float32(alpha),
                             n_groups=n_groups)
    return pl.pallas_call(
        body,
        grid=(n_rows // tm,),
        in_specs=in_specs,
        out_specs=pl.BlockSpec((tm, D_MODEL), lambda i: (i, 0)),
        out_shape=jax.ShapeDtypeStruct((n_rows, D_MODEL), F32),
        scratch_shapes=[pltpu.VMEM((tm, WIDTH), BF16), pltpu.VMEM((tm, WIDTH), BF16)],
        compiler_params=pltpu.CompilerParams(
            dimension_semantics=("arbitrary",), vmem_limit_bytes=VMEM_LIMIT),
        name=name,
    )(*([h] * 8), *o_parts, *lse_parts, x, wmix, bmix, woa, wob, wout, ln_g, ln_b)


def _rope_tables(pos):
    half = HEAD_DIM // 2
    inv = ROPE_THETA ** (-jnp.arange(0, half, dtype=F32) * 2.0 / HEAD_DIM)
    ang = pos.astype(F32)[:, None] * inv[None, :]
    cos = jnp.cos(ang)
    sin = jnp.sin(ang)
    return jnp.concatenate([cos, cos], axis=-1), jnp.concatenate([-sin, sin], axis=-1)


def kernel(x_prompt, x_sample, cache_k_w128, cache_v_w128, cache_k_w512, cache_v_w512,
           cache_k_w2048, cache_v_w2048, w_in, b_in, w_s, b_s, ln_v_g, ln_v_b,
           w_o_a, w_o_b, w_out, ln_g, ln_b):
    depth = w_in.shape[0]
    assert depth == 1, "single-layer step"
    batch, seq, _ = x_prompt.shape
    n_dec, n_new, _ = x_sample.shape
    assert seq == SEQ and n_new == 8 and x_prompt.shape[2] == D_MODEL
    caches_k = (cache_k_w128, cache_k_w512, cache_k_w2048)
    caches_v = (cache_v_w128, cache_v_w512, cache_v_w2048)
    for g, (win, _) in enumerate(DIL_GROUPS):
        assert caches_k[g].shape == (depth, n_dec, win, HEADS, HEAD_DIM)
    past_len = cache_k_w2048.shape[2]
    alpha = float(2 * depth) ** 0.25

    w_f32 = w_in.reshape(D_MODEL, -1)
    b2 = b_in.reshape(1, -1)
    lvg = ln_v_g.reshape(1, WIDTH)
    lvb = ln_v_b.reshape(1, WIDTH)
    lg = ln_g.reshape(1, D_MODEL)
    lb = ln_b.reshape(1, D_MODEL)
    ws = w_s.reshape(HEADS, CHUNK, CHUNK)
    bs = b_s.reshape(HEADS, CHUNK)
    woa = w_o_a.reshape(WIDTH, D_MODEL).astype(BF16)
    wob = w_o_b.reshape(WIDTH, D_MODEL).astype(BF16)
    wout = w_out.reshape(D_MODEL, D_MODEL).astype(BF16)

    cos_p, sin_p = _rope_tables(jnp.arange(seq, dtype=jnp.int32))
    pos_s = past_len + jnp.tile(jnp.arange(n_new, dtype=jnp.int32), n_dec)
    cos_s, sin_s = _rope_tables(pos_s)

    xp2 = x_prompt.reshape(batch * seq, D_MODEL)
    xs2 = x_sample.reshape(n_dec * n_new, D_MODEL)

    pouts = _inproj_prompt(xp2.astype(BF16), w_f32, b2, cos_p, sin_p, lvg, lvb)
    h_kv, h_other = pouts[:2]
    new_pk, new_pv = pouts[2:5], pouts[5:8]
    hs3d, qkv_s, gv_s = _inproj_sample(xs2.astype(BF16), w_f32, b2, cos_s, sin_s, lvg, lvb)
    qkv_s = qkv_s.reshape(3 * N_DIL, n_dec * n_new, HEADS, HEAD_DIM)

    aouts = _attn_sample(qkv_s, caches_k, caches_v, h_kv, h_other, n_dec, n_new)
    o_s, kn_s, vn_s = aouts[0], aouts[1:1 + N_DIL], aouts[1 + N_DIL:1 + 2 * N_DIL]
    o_parts, lse_parts = [aouts[-2]], [aouts[-1]]
    for g in range(1, N_DIL):
        o_g, lse_g = _attn_prompt_group(h_kv, h_other, g, batch)
        o_parts.append(o_g)
        lse_parts.append(lse_g)

    bmix_p = jnp.repeat(bs.T, CHUNK, axis=1)
    yp = _final(h_other, OTHER_COLS, o_parts, lse_parts, xp2, ws, bmix_p, woa, wob, wout, lg, lb,
                CHUNK, alpha, "final_prompt")
    reps = CHUNK // n_new
    wmix_s = jnp.tile(ws[:, :n_new, :n_new], (1, reps, reps))
    bmix_s = jnp.repeat(jnp.tile(bs[:, :n_new], (1, reps)).T, CHUNK, axis=1)
    ys = _final(hs3d, ALL_COLS, [o_s.reshape(n_dec * n_new, WIDTH)], [], xs2, wmix_s, bmix_s,
                woa, wob, wout, lg, lb, n_new, alpha, "final_sample")

    shp_p = lambda a: a.reshape(depth, batch, -1, HEADS, HEAD_DIM)
    new_p = tuple(shp_p(a) for pair in zip(new_pk, new_pv) for a in pair)
    shp_s = lambda a: a.reshape(depth, n_dec, n_new, HEADS, HEAD_DIM)
    new_s = tuple(shp_s(a) for pair in zip(kn_s, vn_s) for a in pair)
    return ((yp.reshape(batch, seq, D_MODEL), ys.reshape(n_dec, n_new, D_MODEL))
            + new_p + new_s + (gv_s.reshape(depth, n_dec, n_new, WIDTH),))
```

```python
import functools

import numpy as np
import jax
import jax.numpy as jnp
from jax import lax
from jax.experimental import pallas as pl
from jax.experimental.pallas import tpu as pltpu

F32 = jnp.float32
BF16 = jnp.bfloat16

D_MODEL = 2048
HEAD_DIM = 128
HEADS = 8
DIL_GROUPS = ((128, 1), (512, 4), (2048, 16))
N_DIL = len(DIL_GROUPS)
BAND = 128
WIDTH = HEADS * HEAD_DIM
CHUNK = 128
SEQ = 2048
ROPE_THETA = 10000.0
LN_EPS = 1e-5
N_COL_TILES = 17
J_K0, J_V0, J_ZB, J_U, J_VA, J_ZA, J_GA, J_GB = 3, 6, 9, 10, 11, 12, 13, 15
ALL_COLS = tuple(range(N_COL_TILES))
KV_COLS = tuple(range(J_K0, J_ZB))
OTHER_COLS = tuple(c for c in ALL_COLS if c not in KV_COLS)
EPILOGUE_ROWS = 64
MXU_COLS = 256
FINAL_ROWS = 256
STAGE_CHUNK = 256
STAGE_HEADS = 4
STAGE_RESIDUES = 2
LSE_REP_LOG2 = 4
LSE_REP = 1 << LSE_REP_LOG2
VMEM_LIMIT = 56 * 1024 * 1024


def _gelu(x):
    return 0.5 * x * (1.0 + lax.erf(x * np.float32(np.sqrt(0.5))))


def _sigmoid(x):
    return 0.5 * (jnp.tanh(0.5 * x) + 1.0)


def _silu(x):
    return x * _sigmoid(x)


def _layer_norm_rows(x, g, b):
    mu = jnp.mean(x, axis=-1, keepdims=True)
    xc = x - mu
    var = jnp.mean(xc * xc, axis=-1, keepdims=True)
    return xc * lax.rsqrt(var + LN_EPS) * g + b


def _col_of(cols, jj):
    cols = list(cols)
    gaps = [k for k in range(1, len(cols)) if cols[k] != cols[k - 1] + 1]
    assert len(gaps) <= 1
    if not gaps:
        return jj + cols[0]
    k = gaps[0]
    return jnp.where(jj < k, jj + cols[0], jj + (cols[k] - k))


def _inproj_body(x_ref, w_ref, b_ref, cos_ref, sin_ref, lng_ref, lnb_ref, *rest,
                 tm, f32_specs, cols):
    h_ref = rest[0]
    f32_refs = rest[1:1 + len(f32_specs)]
    acc_ref, wbf_ref = rest[-2:]
    j = _col_of(cols, pl.program_id(0))
    i = pl.program_id(1)
    has = lambda lo, hi: any(lo <= c < hi for c in cols)
    tiles_per_batch = SEQ // tm
    last_of_batch = i % tiles_per_batch == tiles_per_batch - 1

    @pl.when(i == 0)
    def _():
        def narrow(c, carry):
            rows = pl.ds(pl.multiple_of(c * MXU_COLS, MXU_COLS), MXU_COLS)
            wbf_ref[rows, :] = w_ref[rows, :].astype(BF16)
            return carry
        lax.fori_loop(0, D_MODEL // MXU_COLS, narrow, 0)

    def project(fn, emit_bf16=True):
        for c in range(WIDTH // MXU_COLS):
            cols = slice(c * MXU_COLS, (c + 1) * MXU_COLS)
            a = (jnp.dot(x_ref[...], wbf_ref[:, cols], preferred_element_type=F32)
                 + b_ref[:, cols])
            res = fn(a)
            acc_ref[:, cols] = res
            if emit_bf16:
                h_ref[:, cols] = res.astype(BF16)

    def rope(a):
        cos = cos_ref[...]
        sin = sin_ref[...]
        parts = []
        for h in range(MXU_COLS // HEAD_DIM):
            ah = a[:, h * HEAD_DIM:(h + 1) * HEAD_DIM]
            parts.append(ah * cos + pltpu.roll(ah, HEAD_DIM // 2, 1) * sin)
        return jnp.concatenate(parts, axis=1)

    if has(0, J_V0):
        @pl.when(j < J_V0)
        def _():
            project(rope)

    if has(J_V0, J_ZB):
        @pl.when((j >= J_V0) & (j < J_ZB))
        def _():
            project(lambda a: a)

    if has(J_ZB, J_ZB + 1) or has(J_ZA, J_ZA + 1):
        @pl.when((j == J_ZB) | (j == J_ZA))
        def _():
            project(_silu)

    if has(J_U, J_U + 1):
        @pl.when(j == J_U)
        def _():
            project(_gelu)

    if has(J_VA, J_VA + 1):
        @pl.when(j == J_VA)
        def _():
            project(_gelu, emit_bf16=False)

            def normalise(c, carry):
                rows = pl.ds(pl.multiple_of(c * EPILOGUE_ROWS, EPILOGUE_ROWS), EPILOGUE_ROWS)
                res = _layer_norm_rows(acc_ref[rows, :], lng_ref[...], lnb_ref[...])
                acc_ref[rows, :] = res
                h_ref[rows, :] = res.astype(BF16)
                return carry
            lax.fori_loop(0, tm // EPILOGUE_ROWS, normalise, 0)

    if has(J_GA, N_COL_TILES):
        @pl.when(j >= J_GA)
        def _():
            project(_sigmoid)

    def scatter_heads(o_ref, row0, n_rows):
        for h in range(HEADS):
            o_ref[pl.ds(h, n_rows, stride=HEADS), :] = (
                acc_ref[row0:row0 + n_rows, h * HEAD_DIM:(h + 1) * HEAD_DIM])

    for (j_own, kind), o_ref in zip(f32_specs, f32_refs):
        if kind == 'heads':
            @pl.when(j == j_own)
            def _(o_ref=o_ref):
                scatter_heads(o_ref, 0, tm)
        elif kind == 'heads_below':
            @pl.when(j < j_own)
            def _(o_ref=o_ref):
                scatter_heads(o_ref, 0, tm)
        elif kind == 'heads_last':
            @pl.when((j == j_own) & last_of_batch)
            def _(o_ref=o_ref):
                scatter_heads(o_ref, 0, tm)
        elif kind == 'heads_tail':
            @pl.when((j == j_own) & last_of_batch)
            def _(o_ref=o_ref):
                scatter_heads(o_ref, tm - 128, 128)
        elif kind == 'flat':
            @pl.when(j == j_own)
            def _(o_ref=o_ref):
                o_ref[...] = acc_ref[...]
        else:
            raise ValueError(kind)


def _sticky(j_own, first, last, idx_fn, cols=ALL_COLS):
    def index_map(jj, i):
        j = _col_of(cols, jj)
        idx = idx_fn(i)
        return tuple(jnp.where(j < j_own, f, jnp.where(j > j_own, l, k))
                     for f, l, k in zip(first, last, idx))
    return index_map


def _inproj_call(x_bf, w_f32, b_in, cos_tab, sin_tab, ln_g, ln_b, *, tm, table_tiles,
                 f32_specs, f32_shapes, f32_blocks, name, cols=ALL_COLS):
    n_rows = x_bf.shape[0]
    n_i = n_rows // tm
    col = lambda jj: _col_of(cols, jj)
    in_specs = [
        pl.BlockSpec((tm, D_MODEL), lambda jj, i: (i, 0)),
        pl.BlockSpec((D_MODEL, WIDTH), lambda jj, i: (0, col(jj))),
        pl.BlockSpec((1, WIDTH), lambda jj, i: (0, col(jj))),
        pl.BlockSpec((tm, HEAD_DIM), lambda jj, i: (i % table_tiles, 0)),
        pl.BlockSpec((tm, HEAD_DIM), lambda jj, i: (i % table_tiles, 0)),
        pl.BlockSpec((1, WIDTH), lambda jj, i: (0, 0)),
        pl.BlockSpec((1, WIDTH), lambda jj, i: (0, 0)),
    ]
    out_shapes = [jax.ShapeDtypeStruct((len(cols), n_rows, WIDTH), BF16)] + list(f32_shapes)
    out_specs = ([pl.BlockSpec((None, tm, WIDTH), lambda jj, i: (jj, i, 0))]
                 + list(f32_blocks))
    body = functools.partial(_inproj_body, tm=tm, f32_specs=tuple(f32_specs),
                             cols=tuple(cols))
    return pl.pallas_call(
        body,
        grid=(len(cols), n_i),
        in_specs=in_specs,
        out_specs=out_specs,
        out_shape=out_shapes,
        scratch_shapes=[pltpu.VMEM((tm, WIDTH), F32), pltpu.VMEM((D_MODEL, WIDTH), BF16)],
        compiler_params=pltpu.CompilerParams(
            dimension_semantics=("arbitrary", "arbitrary"), vmem_limit_bytes=VMEM_LIMIT),
        name=name,
    )(x_bf, w_f32, b_in, cos_tab, sin_tab, ln_g, ln_b)


def _inproj_prompt(x_bf, w_f32, b_in, cos_tab, sin_tab, ln_g, ln_b):
    batch = x_bf.shape[0] // SEQ
    kv_cols = KV_COLS
    tm = 512
    tpb = SEQ // tm
    specs, shapes, blocks = [], [], []
    for base in (J_K0, J_V0):
        for g, (win, _) in enumerate(DIL_GROUPS):
            keep = min(win, SEQ)
            shapes.append(jax.ShapeDtypeStruct((batch * keep * HEADS, HEAD_DIM), F32))
            if keep < tm:
                specs.append((base + g, 'heads_tail'))
                idx_fn = lambda i: (i // tpb, 0)
                rows = keep
            elif keep == tm:
                specs.append((base + g, 'heads_last'))
                idx_fn = lambda i: (i // tpb, 0)
                rows = tm
            else:
                assert keep == SEQ
                specs.append((base + g, 'heads'))
                idx_fn = lambda i: (i, 0)
                rows = tm
            last = (batch * keep // rows - 1, 0)
            blocks.append(pl.BlockSpec((rows * HEADS, HEAD_DIM),
                                       _sticky(base + g, (0, 0), last, idx_fn, kv_cols)))
    outs = _inproj_call(x_bf, w_f32, b_in, cos_tab, sin_tab, ln_g, ln_b, tm=tm,
                        table_tiles=tpb, f32_specs=specs, f32_shapes=shapes,
                        f32_blocks=blocks, name="inproj_prompt_kv", cols=kv_cols)
    tm_other = 1024
    h_other = _inproj_call(x_bf, w_f32, b_in, cos_tab, sin_tab, ln_g, ln_b, tm=tm_other,
                           table_tiles=SEQ // tm_other, f32_specs=[], f32_shapes=[],
                           f32_blocks=[], name="inproj_prompt_other", cols=OTHER_COLS)[0]
    return [outs[0], h_other] + list(outs[1:])


def _inproj_sample(x_bf, w_f32, b_in, cos_tab, sin_tab, ln_g, ln_b):
    n_rows = x_bf.shape[0]
    tm = 512
    n_i = n_rows // tm
    specs = [(J_ZB, 'heads_below')]
    shapes = [jax.ShapeDtypeStruct((J_ZB, n_rows * HEADS, HEAD_DIM), F32)]
    blocks = [pl.BlockSpec(
        (None, tm * HEADS, HEAD_DIM),
        lambda j, i: (jnp.minimum(j, J_ZB - 1), jnp.where(j < J_ZB, i, n_i - 1), 0))]
    specs.append((J_VA, 'flat'))
    shapes.append(jax.ShapeDtypeStruct((n_rows, WIDTH), F32))
    blocks.append(pl.BlockSpec((tm, WIDTH),
                               _sticky(J_VA, (0, 0), (n_i - 1, 0), lambda i: (i, 0))))
    return _inproj_call(x_bf, w_f32, b_in, cos_tab, sin_tab, ln_g, ln_b, tm=tm,
                        table_tiles=n_i, f32_specs=specs, f32_shapes=shapes,
                        f32_blocks=blocks, name="inproj_sample")


def _lse_tile(cols):
    rows = cols[0].shape[0]
    lane_head = lax.broadcasted_iota(jnp.int32, (rows, HEADS * LSE_REP), 1) >> LSE_REP_LOG2
    tile = jnp.zeros((rows, HEADS * LSE_REP), F32)
    for h, c in enumerate(cols):
        tile = jnp.where(lane_head == h, c, tile)
    return tile


def _band_softmax(qkvs, off, min_key=None):
    win = qkvs[0][1].shape[0]
    scale = np.float32(HEAD_DIM ** -0.5)
    qi = lax.broadcasted_iota(jnp.int32, (BAND, win), 0)
    ki = lax.broadcasted_iota(jnp.int32, (BAND, win), 1)
    diff = qi - ki + off
    mask = (diff >= 0) & (diff <= BAND)
    if min_key is not None:
        mask = mask & (ki >= min_key)
    scores = [lax.dot_general(q, k, (((1,), (1,)), ((), ())), preferred_element_type=F32)
              for q, k, _ in qkvs]
    probs, dens, lses = [], [], []
    for s in scores:
        s = jnp.where(mask, s * scale, -jnp.inf)
        m = jnp.max(s, axis=-1, keepdims=True)
        p = jnp.exp(s - m)
        den = jnp.sum(p, axis=-1, keepdims=True)
        probs.append(p.astype(BF16))
        dens.append(den)
        lses.append(m + jnp.log(den))
    outs = [jnp.dot(p, v, preferred_element_type=F32) for p, (_, _, v) in zip(probs, qkvs)]
    return [(o / den, lse) for o, den, lse in zip(outs, dens, lses)]


def _attn_prompt_body(q_ref, k_ref, v_ref, o_ref, lse_ref, *scratch, dil):
    sub_len = SEQ // dil
    n_blocks = sub_len // BAND
    nb_log2 = n_blocks.bit_length() - 1

    assert dil > 1, "the undilated group is computed alongside the sample attention"

    qs_ref, ks_ref, vs_ref, os_ref, ls_ref = scratch
    m = STAGE_CHUNK // dil
    m_log2 = m.bit_length() - 1
    n_chunks = SEQ // STAGE_CHUNK
    row = lax.broadcasted_iota(jnp.int32, (STAGE_CHUNK, STAGE_CHUNK), 0)
    col = lax.broadcasted_iota(jnp.int32, (STAGE_CHUNK, STAGE_CHUNK), 1)
    to_staged = col == ((row & (m - 1)) * dil + (row >> m_log2))
    to_natural = row == ((col & (m - 1)) * dil + (col >> m_log2))
    perm = jnp.where(to_staged, 1.0, 0.0).astype(BF16)
    unperm = jnp.where(to_natural, 1.0, 0.0).astype(BF16)

    def staged_rows(c, r):
        return pl.ds(pl.multiple_of(r * sub_len + c * m, m), m)

    def gather_chunk(ref, c):
        return jnp.concatenate([ref[staged_rows(c, r), :] for r in range(dil)], axis=0)

    def natural_rows(c):
        return pl.ds(pl.multiple_of(c * STAGE_CHUNK, STAGE_CHUNK), STAGE_CHUNK)

    for hp in range(HEADS // STAGE_HEADS):
        pass_lanes = slice(hp * STAGE_HEADS * HEAD_DIM, (hp + 1) * STAGE_HEADS * HEAD_DIM)

        def stage(c, carry, pass_lanes=pass_lanes):
            for src, dst in ((q_ref, qs_ref), (k_ref, ks_ref), (v_ref, vs_ref)):
                y = jnp.dot(perm, src[natural_rows(c), pass_lanes],
                            preferred_element_type=F32).astype(BF16)
                for r in range(dil):
                    dst[staged_rows(c, r), :] = y[r * m:(r + 1) * m, :]
            return carry
        lax.fori_loop(0, n_chunks, stage, 0, unroll=4)

        def tile(t, carry, hp=hp):
            n = t & (n_blocks - 1)
            r0 = (t >> nb_log2) * STAGE_RESIDUES
            if n_blocks == 1:
                kb = 0
                off = 0
            else:
                kb = jnp.maximum(n - 1, 0)
                off = (n - kb) * BAND
            key_rows = min(2 * BAND, sub_len)
            head_lanes = [slice(hh * HEAD_DIM, (hh + 1) * HEAD_DIM) for hh in range(STAGE_HEADS)]
            items, dests = [], []
            for rr in range(STAGE_RESIDUES):
                base = (r0 + rr) * sub_len
                q_rows = pl.ds(pl.multiple_of(base + n * BAND, BAND), BAND)
                k_rows = pl.ds(pl.multiple_of(base + kb * BAND, BAND), key_rows)
                for hh, lanes in enumerate(head_lanes):
                    items.append((qs_ref[q_rows, lanes], ks_ref[k_rows, lanes],
                                  vs_ref[k_rows, lanes]))
                    dests.append((q_rows, lanes, hp * STAGE_HEADS + hh))
            for (q_rows, lanes, h), (o, lse) in zip(dests, _band_softmax(items, off)):
                os_ref[q_rows, lanes] = o.astype(BF16)
                ls_ref[q_rows, h * LSE_REP:(h + 1) * LSE_REP] = jnp.broadcast_to(
                    lse, (BAND, LSE_REP))
            return carry
        lax.fori_loop(0, dil * n_blocks // STAGE_RESIDUES, tile, 0)

        def unstage(c, carry, pass_lanes=pass_lanes):
            o_ref[natural_rows(c), pass_lanes] = jnp.dot(
                unperm, gather_chunk(os_ref, c), preferred_element_type=F32).astype(BF16)
            return carry
        lax.fori_loop(0, n_chunks, unstage, 0, unroll=4)

    def unstage_lse(c, carry):
        z = gather_chunk(ls_ref, c)
        hi = z.astype(BF16)
        rest = z - hi.astype(F32)
        mid = rest.astype(BF16)
        lo = (rest - mid.astype(F32)).astype(BF16)
        move = lambda piece: jnp.dot(unperm, piece, preferred_element_type=F32)
        lse_ref[natural_rows(c), :] = move(hi) + (move(mid) + move(lo))
        return carry
    lax.fori_loop(0, n_chunks, unstage_lse, 0, unroll=4)


def _attn_prompt_group(h_kv, h_other, g, batch):
    _, dil = DIL_GROUPS[g]
    tile = (None, SEQ, WIDTH)
    in_specs = [
        pl.BlockSpec(tile, lambda b, g=g: (OTHER_COLS.index(g), b, 0)),
        pl.BlockSpec(tile, lambda b, g=g: (KV_COLS.index(J_K0 + g), b, 0)),
        pl.BlockSpec(tile, lambda b, g=g: (KV_COLS.index(J_V0 + g), b, 0)),
    ]
    scratch = ([pltpu.VMEM((SEQ, STAGE_HEADS * HEAD_DIM), BF16)] * 4
               + [pltpu.VMEM((SEQ, HEADS * LSE_REP), F32)])
    return pl.pallas_call(
        functools.partial(_attn_prompt_body, dil=dil),
        grid=(batch,),
        in_specs=in_specs,
        out_specs=[pl.BlockSpec((SEQ, WIDTH), lambda b: (b, 0)),
                   pl.BlockSpec((SEQ, HEADS * LSE_REP), lambda b: (b, 0))],
        out_shape=[jax.ShapeDtypeStruct((batch * SEQ, WIDTH), BF16),
                   jax.ShapeDtypeStruct((batch * SEQ, HEADS * LSE_REP), F32)],
        scratch_shapes=scratch,
        compiler_params=pltpu.CompilerParams(
            dimension_semantics=("arbitrary",), vmem_limit_bytes=VMEM_LIMIT),
        name="attn_prompt_g%d" % g,
    )(h_other, h_kv, h_kv)


def _attn_sample_body(qkv_ref, ck0_ref, cv0_ref, ck1_ref, cv1_ref, ck2_ref, cv2_ref,
                      pq_ref, pkp_ref, pkc_ref, pvp_ref, pvc_ref, o_ref,
                      nk0_ref, nk1_ref, nk2_ref, nv0_ref, nv1_ref, nv2_ref, po_ref, pl_ref,
                      bc0_ref, bc1_ref, bc2_ref, bn_ref, *, n_new):
    scale = np.float32(HEAD_DIM ** -0.5)
    q_refs, kn_refs, vn_refs = (
        tuple(qkv_ref.at[lo + g] for g in range(N_DIL)) for lo in (0, N_DIL, 2 * N_DIL))
    ck_refs = (ck0_ref, ck1_ref, ck2_ref)
    cv_refs = (cv0_ref, cv1_ref, cv2_ref)
    bc_refs = (bc0_ref, bc1_ref, bc2_ref)
    n_q = n_new * HEADS
    heads_log2 = HEADS.bit_length() - 1

    @pl.when(pl.program_id(0) == 0)
    def _():
        for g, (_, dil) in enumerate(DIL_GROUPS):
            n_res = min(dil, n_new)
            dil_log2 = dil.bit_length() - 1
            res_log2 = n_res.bit_length() - 1
            n_keys = BAND * n_res * HEADS
            row = lax.broadcasted_iota(jnp.int32, (n_q, n_keys), 0)
            col = lax.broadcasted_iota(jnp.int32, (n_q, n_keys), 1)
            t = row >> heads_log2
            ok = (((col & (HEADS - 1)) == (row & (HEADS - 1)))
                  & (((col >> heads_log2) & (n_res - 1)) == (t & (dil - 1)))
                  & ((col >> (heads_log2 + res_log2)) >= (t >> dil_log2)))
            bc_refs[g][...] = jnp.where(ok, 0.0, -jnp.inf).astype(F32)
            row = lax.broadcasted_iota(jnp.int32, (n_q, n_q), 0)
            col = lax.broadcasted_iota(jnp.int32, (n_q, n_q), 1)
            dn = (row >> heads_log2) - (col >> heads_log2)
            ok = (((col & (HEADS - 1)) == (row & (HEADS - 1)))
                  & (dn >= 0) & ((dn & (dil - 1)) == 0))
            bn_ref[g] = jnp.where(ok, 0.0, -jnp.inf).astype(F32)

    contract_last = (((1,), (1,)), ((), ()))
    outs, lses = [], []
    for g, (_, dil) in enumerate(DIL_GROUPS):
        n_keys = BAND * min(dil, n_new) * HEADS
        q = q_refs[g][...].reshape(n_q, HEAD_DIM).astype(BF16)
        kn = kn_refs[g][...].reshape(n_q, HEAD_DIM).astype(BF16)
        vn = vn_refs[g][...].reshape(n_q, HEAD_DIM).astype(BF16)
        kc = ck_refs[g][...].reshape(n_keys, HEAD_DIM).astype(BF16)
        vc = cv_refs[g][...].reshape(n_keys, HEAD_DIM).astype(BF16)
        s = lax.dot_general(q, kc, contract_last, preferred_element_type=F32) * scale
        s = s + bc_refs[g][...]
        s_new = lax.dot_general(q, kn, contract_last, preferred_element_type=F32) * scale
        s_new = s_new + bn_ref[g]
        m = jnp.maximum(jnp.max(s, axis=-1, keepdims=True),
                        jnp.max(s_new, axis=-1, keepdims=True))
        p = jnp.exp(s - m)
        p_new = jnp.exp(s_new - m)
        den = jnp.sum(p, axis=-1, keepdims=True) + jnp.sum(p_new, axis=-1, keepdims=True)
        acc = (jnp.dot(p.astype(BF16), vc, preferred_element_type=F32)
               + jnp.dot(p_new.astype(BF16), vn, preferred_element_type=F32))
        outs.append(acc / den)
        lses.append(m + jnp.log(den))
    mx = jnp.maximum(jnp.maximum(lses[0], lses[1]), lses[2])
    ws = [jnp.exp(l - mx) for l in lses]
    tot = ws[0] + ws[1] + ws[2]
    o = (ws[0] * outs[0] + ws[1] * outs[1] + ws[2] * outs[2]) / tot
    o_ref[...] = o.reshape(n_new, HEADS, HEAD_DIM)
    for dst, src in zip((nk0_ref, nk1_ref, nk2_ref, nv0_ref, nv1_ref, nv2_ref),
                        kn_refs + vn_refs):
        dst[...] = src[...]

    first_block = (pl.program_id(0) & (SEQ // BAND - 1)) == 0
    min_key = jnp.where(first_block, BAND, 0)
    head_lanes = [slice(h * HEAD_DIM, (h + 1) * HEAD_DIM) for h in range(HEADS)]
    res = _band_softmax(
        [(pq_ref[:, lanes],
          jnp.concatenate([pkp_ref[:, lanes], pkc_ref[:, lanes]], axis=0),
          jnp.concatenate([pvp_ref[:, lanes], pvc_ref[:, lanes]], axis=0))
         for lanes in head_lanes], BAND, min_key)
    for lanes, (o_blk, _) in zip(head_lanes, res):
        po_ref[:, lanes] = o_blk.astype(BF16)
    pl_ref[...] = _lse_tile([lse for _, lse in res])


def _attn_sample(qkv, caches_k, caches_v, h_kv, h_other, n_batch, n_new):
    row_block = (n_new, HEADS, HEAD_DIM)
    in_specs = [pl.BlockSpec((3 * N_DIL,) + row_block, lambda b: (0, b, 0, 0))]
    args = [qkv]
    n_q = n_new * HEADS
    scratch = []
    for g, (win, dil) in enumerate(DIL_GROUPS):
        n_res = min(dil, n_new)
        for c in (caches_k[g], caches_v[g]):
            in_specs.append(pl.BlockSpec((None, BAND, n_res, HEADS, HEAD_DIM),
                                         lambda b: (b, 0, 0, 0, 0)))
            args.append(c.reshape(n_batch, win // dil, dil, HEADS, HEAD_DIM))
        scratch.append(pltpu.VMEM((n_q, BAND * n_res * HEADS), F32))
    scratch.append(pltpu.VMEM((N_DIL, n_q, n_q), F32))
    blocks_per_seq = SEQ // BAND
    n_prompt_rows = h_kv.shape[1]
    assert n_prompt_rows == n_batch * BAND, "one prompt query block per sample step"
    k_idx, v_idx = KV_COLS.index(J_K0), KV_COLS.index(J_V0)
    prev = lambda s: jnp.where((s & (blocks_per_seq - 1)) == 0, s, s - 1)
    tile = (None, BAND, WIDTH)
    in_specs += [pl.BlockSpec(tile, lambda s: (OTHER_COLS.index(0), s, 0)),
                 pl.BlockSpec(tile, lambda s: (k_idx, prev(s), 0)),
                 pl.BlockSpec(tile, lambda s: (k_idx, s, 0)),
                 pl.BlockSpec(tile, lambda s: (v_idx, prev(s), 0)),
                 pl.BlockSpec(tile, lambda s: (v_idx, s, 0))]
    args += [h_other] + [h_kv] * 4
    row_spec = pl.BlockSpec(row_block, lambda b: (b, 0, 0))
    n_out = 1 + 2 * N_DIL
    body = functools.partial(_attn_sample_body, n_new=n_new)
    return pl.pallas_call(
        body,
        grid=(n_batch,),
        in_specs=in_specs,
        out_specs=[row_spec] * n_out + [pl.BlockSpec((BAND, WIDTH), lambda s: (s, 0)),
                                        pl.BlockSpec((BAND, HEADS * LSE_REP), lambda s: (s, 0))],
        out_shape=([jax.ShapeDtypeStruct((n_batch * n_new,) + row_block[1:], F32)] * n_out
                   + [jax.ShapeDtypeStruct((n_prompt_rows, WIDTH), BF16),
                      jax.ShapeDtypeStruct((n_prompt_rows, HEADS * LSE_REP), F32)]),
        scratch_shapes=scratch,
        compiler_params=pltpu.CompilerParams(
            dimension_semantics=("arbitrary",), vmem_limit_bytes=VMEM_LIMIT),
        name="attn_sample",
    )(*args)


def _final_body(*refs, tm, mix_rows, alpha, n_groups):
    zb_ref, gu_ref, vn_ref, za_ref, ga0_ref, ga1_ref, gb0_ref, gb1_ref = refs[:8]
    refs = refs[8:]
    if n_groups:
        og_refs = refs[:n_groups]
        lg_refs = refs[n_groups:2 * n_groups]
        refs = refs[2 * n_groups:]
    else:
        o_ref = refs[0]
        refs = refs[1:]
    (x_ref, wmix_ref, bmix_ref, woa_ref, wob_ref, wout_ref, lng_ref, lnb_ref,
     y_ref, ya_ref, yb_ref) = refs

    if n_groups:
        lses = [l_ref[...] for l_ref in lg_refs]
        mx = functools.reduce(jnp.maximum, lses)
        ws = [jnp.exp(l - mx) for l in lses]
        tot = functools.reduce(lambda a, b: a + b, ws)
        ws = [w / tot for w in ws]
        for h in range(HEADS):
            lanes = slice(h * HEAD_DIM, (h + 1) * HEAD_DIM)
            o = None
            for w, og_ref in zip(ws, og_refs):
                term = w[:, h * LSE_REP:h * LSE_REP + 1] * og_ref[:, lanes].astype(F32)
                o = term if o is None else o + term
            yb_ref[:, lanes] = (o * zb_ref[:, lanes].astype(F32)).astype(BF16)
    else:
        yb_ref[...] = (o_ref[...] * zb_ref[...].astype(F32)).astype(BF16)

    row = lax.broadcasted_iota(jnp.int32, (CHUNK, CHUNK), 0)
    col = lax.broadcasted_iota(jnp.int32, (CHUNK, CHUNK), 1)
    mix_log2 = mix_rows.bit_length() - 1
    causal = (row >= col) & ((row >> mix_log2) == (col >> mix_log2))
    for g in range(HEADS):
        lanes = slice(g * CHUNK, (g + 1) * CHUNK)
        w_c = jnp.where(causal, wmix_ref[g], 0.0).astype(BF16)
        for c in range(tm // CHUNK):
            rows = slice(c * CHUNK, (c + 1) * CHUNK)
            mixed = jnp.dot(w_c, vn_ref[rows, lanes], preferred_element_type=F32)
            mixed = mixed + bmix_ref[:, lanes]
            ya = gu_ref[rows, lanes].astype(F32) * mixed * za_ref[rows, lanes].astype(F32)
            ya_ref[rows, lanes] = ya.astype(BF16)

    proj_a = jnp.dot(ya_ref[...], woa_ref[...], preferred_element_type=F32)
    proj_b = jnp.dot(yb_ref[...], wob_ref[...], preferred_element_type=F32)
    ga = jnp.concatenate([ga0_ref[...], ga1_ref[...]], axis=1).astype(F32)
    gb = jnp.concatenate([gb0_ref[...], gb1_ref[...]], axis=1).astype(F32)
    merged = (ga * proj_a + gb * proj_b).astype(BF16)
    z = alpha * x_ref[...] + jnp.dot(merged, wout_ref[...], preferred_element_type=F32)
    y_ref[...] = _layer_norm_rows(z, lng_ref[...], lnb_ref[...])


def _final(h, h_cols, o_parts, lse_parts, x, wmix, bmix, woa, wob, wout, ln_g, ln_b,
           mix_rows, alpha, name):
    n_rows = x.shape[0]
    tm = FINAL_ROWS
    n_groups = len(lse_parts)
    hspec = lambda j: pl.BlockSpec((None, tm, WIDTH),
                                   lambda i, k=h_cols.index(j): (k, i, 0))
    rowspec = lambda width: pl.BlockSpec((tm, width), lambda i: (i, 0))
    const = lambda shape: pl.BlockSpec(shape, lambda i: (0,) * len(shape),
                                       pipeline_mode=pl.Buffered(1))
    in_specs = [hspec(J_ZB), hspec(J_U), hspec(J_VA), hspec(J_ZA),
                hspec(J_GA), hspec(J_GA + 1), hspec(J_GB), hspec(J_GB + 1)]
    in_specs += [rowspec(WIDTH)] * len(o_parts)
    in_specs += [rowspec(HEADS * LSE_REP)] * n_groups
    in_specs += [rowspec(D_MODEL),
                 const((HEADS, CHUNK, CHUNK)), const((CHUNK, WIDTH)),
                 const((WIDTH, D_MODEL)), const((WIDTH, D_MODEL)), const((D_MODEL, D_MODEL)),
                 const((1, D_MODEL)), const((1, D_MODEL))]
    body = functools.partial(_final_body, tm=tm, mix_rows=mix_rows, alpha=np.float32(alpha),
                             n_groups=n_groups)
    return pl.pallas_call(
        body,
        grid=(n_rows // tm,),
        in_specs=in_specs,
        out_specs=pl.BlockSpec((tm, D_MODEL), lambda i: (i, 0)),
        out_shape=jax.ShapeDtypeStruct((n_rows, D_MODEL), F32),
        scratch_shapes=[pltpu.VMEM((tm, WIDTH), BF16), pltpu.VMEM((tm, WIDTH), BF16)],
        compiler_params=pltpu.CompilerParams(
            dimension_semantics=("arbitrary",), vmem_limit_bytes=VMEM_LIMIT),
        name=name,
    )(*([h] * 8), *o_parts, *lse_parts, x, wmix, bmix, woa, wob, wout, ln_g, ln_b)


def _rope_tables(pos):
    half = HEAD_DIM // 2
    inv = ROPE_THETA ** (-jnp.arange(0, half, dtype=F32) * 2.0 / HEAD_DIM)
    ang = pos.astype(F32)[:, None] * inv[None, :]
    cos = jnp.cos(ang)
    sin = jnp.sin(ang)
    return jnp.concatenate([cos, cos], axis=-1), jnp.concatenate([-sin, sin], axis=-1)


def kernel(x_prompt, x_sample, cache_k_w128, cache_v_w128, cache_k_w512, cache_v_w512,
           cache_k_w2048, cache_v_w2048, w_in, b_in, w_s, b_s, ln_v_g, ln_v_b,
           w_o_a, w_o_b, w_out, ln_g, ln_b):
    depth = w_in.shape[0]
    assert depth == 1, "single-layer step"
    batch, seq, _ = x_prompt.shape
    n_dec, n_new, _ = x_sample.shape
    assert seq == SEQ and n_new == 8 and x_prompt.shape[2] == D_MODEL
    caches_k = (cache_k_w128, cache_k_w512, cache_k_w2048)
    caches_v = (cache_v_w128, cache_v_w512, cache_v_w2048)
    for g, (win, _) in enumerate(DIL_GROUPS):
        assert caches_k[g].shape == (depth, n_dec, win, HEADS, HEAD_DIM)
    past_len = cache_k_w2048.shape[2]
    alpha = float(2 * depth) ** 0.25

    w_f32 = w_in.reshape(D_MODEL, -1)
    b2 = b_in.reshape(1, -1)
    lvg = ln_v_g.reshape(1, WIDTH)
    lvb = ln_v_b.reshape(1, WIDTH)
    lg = ln_g.reshape(1, D_MODEL)
    lb = ln_b.reshape(1, D_MODEL)
    ws = w_s.reshape(HEADS, CHUNK, CHUNK)
    bs = b_s.reshape(HEADS, CHUNK)
    woa = w_o_a.reshape(WIDTH, D_MODEL).astype(BF16)
    wob = w_o_b.reshape(WIDTH, D_MODEL).astype(BF16)
    wout = w_out.reshape(D_MODEL, D_MODEL).astype(BF16)

    cos_p, sin_p = _rope_tables(jnp.arange(seq, dtype=jnp.int32))
    pos_s = past_len + jnp.tile(jnp.arange(n_new, dtype=jnp.int32), n_dec)
    cos_s, sin_s = _rope_tables(pos_s)

    xp2 = x_prompt.reshape(batch * seq, D_MODEL)
    xs2 = x_sample.reshape(n_dec * n_new, D_MODEL)

    pouts = _inproj_prompt(xp2.astype(BF16), w_f32, b2, cos_p, sin_p, lvg, lvb)
    h_kv, h_other = pouts[:2]
    new_pk, new_pv = pouts[2:5], pouts[5:8]
    hs3d, qkv_s, gv_s = _inproj_sample(xs2.astype(BF16), w_f32, b2, cos_s, sin_s, lvg, lvb)
    qkv_s = qkv_s.reshape(3 * N_DIL, n_dec * n_new, HEADS, HEAD_DIM)

    aouts = _attn_sample(qkv_s, caches_k, caches_v, h_kv, h_other, n_dec, n_new)
    o_s, kn_s, vn_s = aouts[0], aouts[1:1 + N_DIL], aouts[1 + N_DIL:1 + 2 * N_DIL]
    o_parts, lse_parts = [aouts[-2]], [aouts[-1]]
    for g in range(1, N_DIL):
        o_g, lse_g = _attn_prompt_group(h_kv, h_other, g, batch)
        o_parts.append(o_g)
        lse_parts.append(lse_g)

    bmix_p = jnp.repeat(bs.T, CHUNK, axis=1)
    yp = _final(h_other, OTHER_COLS, o_parts, lse_parts, xp2, ws, bmix_p, woa, wob, wout, lg, lb,
                CHUNK, alpha, "final_prompt")
    reps = CHUNK // n_new
    wmix_s = jnp.tile(ws[:, :n_new, :n_new], (1, reps, reps))
    bmix_s = jnp.repeat(jnp.tile(bs[:, :n_new], (1, reps)).T, CHUNK, axis=1)
    ys = _final(hs3d, ALL_COLS, [o_s.reshape(n_dec * n_new, WIDTH)], [], xs2, wmix_s, bmix_s,
                woa, wob, wout, lg, lb, n_new, alpha, "final_sample")

    shp_p = lambda a: a.reshape(depth, batch, -1, HEADS, HEAD_DIM)
    new_p = tuple(shp_p(a) for pair in zip(new_pk, new_pv) for a in pair)
    shp_s = lambda a: a.reshape(depth, n_dec, n_new, HEADS, HEAD_DIM)
    new_s = tuple(shp_s(a) for pair in zip(kn_s, vn_s) for a in pair)
    return ((yp.reshape(batch, seq, D_MODEL), ys.reshape(n_dec, n_new, D_MODEL))
            + new_p + new_s + (gv_s.reshape(depth, n_dec, n_new, WIDTH),))
```

```python
import functools

import numpy as np
import jax
import jax.numpy as jnp
from jax import lax
from jax.experimental import pallas as pl
from jax.experimental.pallas import tpu as pltpu

F32 = jnp.float32
BF16 = jnp.bfloat16

D_MODEL = 2048
HEAD_DIM = 128
HEADS = 8
DIL_GROUPS = ((128, 1), (512, 4), (2048, 16))
N_DIL = len(DIL_GROUPS)
BAND = 128
WIDTH = HEADS * HEAD_DIM
CHUNK = 128
SEQ = 2048
ROPE_THETA = 10000.0
LN_EPS = 1e-5
N_COL_TILES = 17
J_K0, J_V0, J_ZB, J_U, J_VA, J_ZA, J_GA, J_GB = 3, 6, 9, 10, 11, 12, 13, 15
ALL_COLS = tuple(range(N_COL_TILES))
KV2_COLS = (J_K0 + 2, J_V0 + 2)
KV01_COLS = (J_K0, J_K0 + 1, J_V0, J_V0 + 1)
OTHER_COLS = tuple(c for c in ALL_COLS if c not in KV2_COLS + KV01_COLS)
EPILOGUE_ROWS = 64
MXU_COLS = 256
FINAL_ROWS = 256
STAGE_CHUNK = 256
STAGE_HEADS = 4
STAGE_RESIDUES = 2
LSE_REP_LOG2 = 4
LSE_REP = 1 << LSE_REP_LOG2
VMEM_LIMIT = 56 * 1024 * 1024


def _gelu(x):
    return 0.5 * x * (1.0 + lax.erf(x * np.float32(np.sqrt(0.5))))


def _sigmoid(x):
    return 0.5 * (jnp.tanh(0.5 * x) + 1.0)


def _silu(x):
    return x * _sigmoid(x)


def _layer_norm_rows(x, g, b):
    mu = jnp.mean(x, axis=-1, keepdims=True)
    xc = x - mu
    var = jnp.mean(xc * xc, axis=-1, keepdims=True)
    return xc * lax.rsqrt(var + LN_EPS) * g + b


def _col_of(cols, jj):
    cols = list(cols)
    gaps = [k for k in range(1, len(cols)) if cols[k] != cols[k - 1] + 1]
    assert len(gaps) <= 1
    if not gaps:
        return jj + cols[0]
    k = gaps[0]
    return jnp.where(jj < k, jj + cols[0], jj + (cols[k] - k))


def _inproj_body(x_ref, w_ref, b_ref, cos_ref, sin_ref, lng_ref, lnb_ref, *rest,
                 tm, f32_specs, cols):
    h_ref = rest[0]
    f32_refs = rest[1:1 + len(f32_specs)]
    acc_ref, wbf_ref = rest[-2:]
    j = _col_of(cols, pl.program_id(0))
    i = pl.program_id(1)
    has = lambda lo, hi: any(lo <= c < hi for c in cols)
    tiles_per_batch = SEQ // tm
    last_of_batch = i % tiles_per_batch == tiles_per_batch - 1

    @pl.when(i == 0)
    def _():
        def narrow(c, carry):
            rows = pl.ds(pl.multiple_of(c * MXU_COLS, MXU_COLS), MXU_COLS)
            wbf_ref[rows, :] = w_ref[rows, :].astype(BF16)
            return carry
        lax.fori_loop(0, D_MODEL // MXU_COLS, narrow, 0)

    def project(fn, emit_bf16=True):
        for c in range(WIDTH // MXU_COLS):
            cols = slice(c * MXU_COLS, (c + 1) * MXU_COLS)
            a = (jnp.dot(x_ref[...], wbf_ref[:, cols], preferred_element_type=F32)
                 + b_ref[:, cols])
            res = fn(a)
            acc_ref[:, cols] = res
            if emit_bf16:
                h_ref[:, cols] = res.astype(BF16)

    def rope(a):
        cos = cos_ref[...]
        sin = sin_ref[...]
        parts = []
        for h in range(MXU_COLS // HEAD_DIM):
            ah = a[:, h * HEAD_DIM:(h + 1) * HEAD_DIM]
            parts.append(ah * cos + pltpu.roll(ah, HEAD_DIM // 2, 1) * sin)
        return jnp.concatenate(parts, axis=1)

    if has(0, J_V0):
        @pl.when(j < J_V0)
        def _():
            project(rope)

    if has(J_V0, J_ZB):
        @pl.when((j >= J_V0) & (j < J_ZB))
        def _():
            project(lambda a: a)

    if has(J_ZB, J_ZB + 1) or has(J_ZA, J_ZA + 1):
        @pl.when((j == J_ZB) | (j == J_ZA))
        def _():
            project(_silu)

    if has(J_U, J_U + 1):
        @pl.when(j == J_U)
        def _():
            project(_gelu)

    if has(J_VA, J_VA + 1):
        @pl.when(j == J_VA)
        def _():
            project(_gelu, emit_bf16=False)

            def normalise(c, carry):
                rows = pl.ds(pl.multiple_of(c * EPILOGUE_ROWS, EPILOGUE_ROWS), EPILOGUE_ROWS)
                res = _layer_norm_rows(acc_ref[rows, :], lng_ref[...], lnb_ref[...])
                acc_ref[rows, :] = res
                h_ref[rows, :] = res.astype(BF16)
                return carry
            lax.fori_loop(0, tm // EPILOGUE_ROWS, normalise, 0)

    if has(J_GA, N_COL_TILES):
        @pl.when(j >= J_GA)
        def _():
            project(_sigmoid)

    def scatter_heads(o_ref, row0, n_rows):
        for h in range(HEADS):
            o_ref[pl.ds(h, n_rows, stride=HEADS), :] = (
                acc_ref[row0:row0 + n_rows, h * HEAD_DIM:(h + 1) * HEAD_DIM])

    for (j_own, kind, rows), o_ref in zip(f32_specs, f32_refs):
        if kind == 'heads':
            @pl.when(j == j_own)
            def _(o_ref=o_ref):
                scatter_heads(o_ref, 0, tm)
        elif kind == 'heads_below':
            @pl.when(j < j_own)
            def _(o_ref=o_ref):
                scatter_heads(o_ref, 0, tm)
        elif kind == 'heads_tail':
            @pl.when((j == j_own) & last_of_batch)
            def _(o_ref=o_ref, rows=rows):
                scatter_heads(o_ref, tm - rows, rows)
        elif kind == 'flat':
            @pl.when(j == j_own)
            def _(o_ref=o_ref):
                o_ref[...] = acc_ref[...]
        else:
            raise ValueError(kind)


def _sticky(j_own, first, last, idx_fn, cols=ALL_COLS):
    def index_map(jj, i):
        j = _col_of(cols, jj)
        idx = idx_fn(i)
        return tuple(jnp.where(j < j_own, f, jnp.where(j > j_own, l, k))
                     for f, l, k in zip(first, last, idx))
    return index_map


def _inproj_call(x_bf, w_f32, b_in, cos_tab, sin_tab, ln_g, ln_b, *, tm, table_tiles,
                 f32_specs, f32_shapes, f32_blocks, name, cols=ALL_COLS):
    n_rows = x_bf.shape[0]
    n_i = n_rows // tm
    col = lambda jj: _col_of(cols, jj)
    in_specs = [
        pl.BlockSpec((tm, D_MODEL), lambda jj, i: (i, 0)),
        pl.BlockSpec((D_MODEL, WIDTH), lambda jj, i: (0, col(jj))),
        pl.BlockSpec((1, WIDTH), lambda jj, i: (0, col(jj))),
        pl.BlockSpec((tm, HEAD_DIM), lambda jj, i: (i % table_tiles, 0)),
        pl.BlockSpec((tm, HEAD_DIM), lambda jj, i: (i % table_tiles, 0)),
        pl.BlockSpec((1, WIDTH), lambda jj, i: (0, 0)),
        pl.BlockSpec((1, WIDTH), lambda jj, i: (0, 0)),
    ]
    out_shapes = [jax.ShapeDtypeStruct((len(cols), n_rows, WIDTH), BF16)] + list(f32_shapes)
    out_specs = ([pl.BlockSpec((None, tm, WIDTH), lambda jj, i: (jj, i, 0))]
                 + list(f32_blocks))
    body = functools.partial(_inproj_body, tm=tm, f32_specs=tuple(f32_specs),
                             cols=tuple(cols))
    return pl.pallas_call(
        body,
        grid=(len(cols), n_i),
        in_specs=in_specs,
        out_specs=out_specs,
        out_shape=out_shapes,
        scratch_shapes=[pltpu.VMEM((tm, WIDTH), F32), pltpu.VMEM((D_MODEL, WIDTH), BF16)],
        compiler_params=pltpu.CompilerParams(
            dimension_semantics=("arbitrary", "arbitrary"), vmem_limit_bytes=VMEM_LIMIT),
        name=name,
    )(x_bf, w_f32, b_in, cos_tab, sin_tab, ln_g, ln_b)


def _inproj_prompt(x_bf, w_f32, b_in, cos_tab, sin_tab, ln_g, ln_b):
    batch = x_bf.shape[0] // SEQ

    def call(cols, tm, name):
        tpb = SEQ // tm
        specs, shapes, blocks = [], [], []
        for c in cols:
            if not J_K0 <= c < J_ZB:
                continue
            keep = min(DIL_GROUPS[(c - J_K0) % N_DIL][0], SEQ)
            shapes.append(jax.ShapeDtypeStruct((batch * keep * HEADS, HEAD_DIM), F32))
            if keep == SEQ:
                specs.append((c, 'heads', tm))
                idx_fn = lambda i: (i, 0)
                rows = tm
            else:
                assert keep <= tm
                specs.append((c, 'heads_tail', keep))
                idx_fn = lambda i, tpb=tpb: (i // tpb, 0)
                rows = keep
            last = (batch * keep // rows - 1, 0)
            blocks.append(pl.BlockSpec((rows * HEADS, HEAD_DIM),
                                       _sticky(c, (0, 0), last, idx_fn, cols)))
        outs = _inproj_call(x_bf, w_f32, b_in, cos_tab, sin_tab, ln_g, ln_b, tm=tm,
                            table_tiles=tpb, f32_specs=specs, f32_shapes=shapes,
                            f32_blocks=blocks, name=name, cols=cols)
        tiles = {c: (outs[0], k) for k, c in enumerate(cols)}
        f32 = {spec[0]: out for spec, out in zip(specs, outs[1:])}
        return tiles, f32

    tiles, f32 = {}, {}
    for cols, tm, name in ((KV2_COLS, 512, "inproj_prompt_kv2"),
                           (KV01_COLS, 1024, "inproj_prompt_kv01"),
                           (OTHER_COLS, 1024, "inproj_prompt_other")):
        t, f = call(cols, tm, name)
        tiles.update(t)
        f32.update(f)
    new_k = [f32[J_K0 + g] for g in range(N_DIL)]
    new_v = [f32[J_V0 + g] for g in range(N_DIL)]
    return tiles, new_k, new_v


def _inproj_sample(x_bf, w_f32, b_in, cos_tab, sin_tab, ln_g, ln_b):
    n_rows = x_bf.shape[0]
    tm = 512
    n_i = n_rows // tm
    specs = [(J_ZB, 'heads_below', tm)]
    shapes = [jax.ShapeDtypeStruct((J_ZB, n_rows * HEADS, HEAD_DIM), F32)]
    blocks = [pl.BlockSpec(
        (None, tm * HEADS, HEAD_DIM),
        lambda j, i: (jnp.minimum(j, J_ZB - 1), jnp.where(j < J_ZB, i, n_i - 1), 0))]
    specs.append((J_VA, 'flat', tm))
    shapes.append(jax.ShapeDtypeStruct((n_rows, WIDTH), F32))
    blocks.append(pl.BlockSpec((tm, WIDTH),
                               _sticky(J_VA, (0, 0), (n_i - 1, 0), lambda i: (i, 0))))
    return _inproj_call(x_bf, w_f32, b_in, cos_tab, sin_tab, ln_g, ln_b, tm=tm,
                        table_tiles=n_i, f32_specs=specs, f32_shapes=shapes,
                        f32_blocks=blocks, name="inproj_sample")


def _lse_tile(cols):
    rows = cols[0].shape[0]
    lane_head = lax.broadcasted_iota(jnp.int32, (rows, HEADS * LSE_REP), 1) >> LSE_REP_LOG2
    tile = jnp.zeros((rows, HEADS * LSE_REP), F32)
    for h, c in enumerate(cols):
        tile = jnp.where(lane_head == h, c, tile)
    return tile


def _band_softmax(qkvs, off, min_key=None):
    win = qkvs[0][1].shape[0]
    scale = np.float32(HEAD_DIM ** -0.5)
    qi = lax.broadcasted_iota(jnp.int32, (BAND, win), 0)
    ki = lax.broadcasted_iota(jnp.int32, (BAND, win), 1)
    diff = qi - ki + off
    mask = (diff >= 0) & (diff <= BAND)
    if min_key is not None:
        mask = mask & (ki >= min_key)
    scores = [lax.dot_general(q, k, (((1,), (1,)), ((), ())), preferred_element_type=F32)
              for q, k, _ in qkvs]
    probs, dens, lses = [], [], []
    for s in scores:
        s = jnp.where(mask, s * scale, -jnp.inf)
        m = jnp.max(s, axis=-1, keepdims=True)
        p = jnp.exp(s - m)
        den = jnp.sum(p, axis=-1, keepdims=True)
        probs.append(p.astype(BF16))
        dens.append(den)
        lses.append(m + jnp.log(den))
    outs = [jnp.dot(p, v, preferred_element_type=F32) for p, (_, _, v) in zip(probs, qkvs)]
    return [(o / den, lse) for o, den, lse in zip(outs, dens, lses)]


def _attn_prompt_body(q_ref, k_ref, v_ref, o_ref, lse_ref, *scratch, dil):
    sub_len = SEQ // dil
    n_blocks = sub_len // BAND
    nb_log2 = n_blocks.bit_length() - 1

    assert dil > 1, "the undilated group is computed alongside the sample attention"

    qs_ref, ks_ref, vs_ref, os_ref, ls_ref = scratch
    m = STAGE_CHUNK // dil
    m_log2 = m.bit_length() - 1
    n_chunks = SEQ // STAGE_CHUNK
    row = lax.broadcasted_iota(jnp.int32, (STAGE_CHUNK, STAGE_CHUNK), 0)
    col = lax.broadcasted_iota(jnp.int32, (STAGE_CHUNK, STAGE_CHUNK), 1)
    to_staged = col == ((row & (m - 1)) * dil + (row >> m_log2))
    to_natural = row == ((col & (m - 1)) * dil + (col >> m_log2))
    perm = jnp.where(to_staged, 1.0, 0.0).astype(BF16)
    unperm = jnp.where(to_natural, 1.0, 0.0).astype(BF16)

    def staged_rows(c, r):
        return pl.ds(pl.multiple_of(r * sub_len + c * m, m), m)

    def gather_chunk(ref, c):
        return jnp.concatenate([ref[staged_rows(c, r), :] for r in range(dil)], axis=0)

    def natural_rows(c):
        return pl.ds(pl.multiple_of(c * STAGE_CHUNK, STAGE_CHUNK), STAGE_CHUNK)

    for hp in range(HEADS // STAGE_HEADS):
        pass_lanes = slice(hp * STAGE_HEADS * HEAD_DIM, (hp + 1) * STAGE_HEADS * HEAD_DIM)

        def stage(c, carry, pass_lanes=pass_lanes):
            for src, dst in ((q_ref, qs_ref), (k_ref, ks_ref), (v_ref, vs_ref)):
                y = jnp.dot(perm, src[natural_rows(c), pass_lanes],
                            preferred_element_type=F32).astype(BF16)
                for r in range(dil):
                    dst[staged_rows(c, r), :] = y[r * m:(r + 1) * m, :]
            return carry
        lax.fori_loop(0, n_chunks, stage, 0, unroll=4)

        def tile(t, carry, hp=hp):
            n = t & (n_blocks - 1)
            r0 = (t >> nb_log2) * STAGE_RESIDUES
            if n_blocks == 1:
                kb = 0
                off = 0
            else:
                kb = jnp.maximum(n - 1, 0)
                off = (n - kb) * BAND
            key_rows = min(2 * BAND, sub_len)
            head_lanes = [slice(hh * HEAD_DIM, (hh + 1) * HEAD_DIM) for hh in range(STAGE_HEADS)]
            items, dests = [], []
            for rr in range(STAGE_RESIDUES):
                base = (r0 + rr) * sub_len
                q_rows = pl.ds(pl.multiple_of(base + n * BAND, BAND), BAND)
                k_rows = pl.ds(pl.multiple_of(base + kb * BAND, BAND), key_rows)
                for hh, lanes in enumerate(head_lanes):
                    items.append((qs_ref[q_rows, lanes], ks_ref[k_rows, lanes],
                                  vs_ref[k_rows, lanes]))
                    dests.append((q_rows, lanes, hp * STAGE_HEADS + hh))
            for (q_rows, lanes, h), (o, lse) in zip(dests, _band_softmax(items, off)):
                os_ref[q_rows, lanes] = o.astype(BF16)
                ls_ref[q_rows, h * LSE_REP:(h + 1) * LSE_REP] = jnp.broadcast_to(
                    lse, (BAND, LSE_REP))
            return carry
        lax.fori_loop(0, dil * n_blocks // STAGE_RESIDUES, tile, 0)

        def unstage(c, carry, pass_lanes=pass_lanes):
            o_ref[natural_rows(c), pass_lanes] = jnp.dot(
                unperm, gather_chunk(os_ref, c), preferred_element_type=F32).astype(BF16)
            return carry
        lax.fori_loop(0, n_chunks, unstage, 0, unroll=4)

    def unstage_lse(c, carry):
        z = gather_chunk(ls_ref, c)
        hi = z.astype(BF16)
        rest = z - hi.astype(F32)
        mid = rest.astype(BF16)
        lo = (rest - mid.astype(F32)).astype(BF16)
        move = lambda piece: jnp.dot(unperm, piece, preferred_element_type=F32)
        lse_ref[natural_rows(c), :] = move(hi) + (move(mid) + move(lo))
        return carry
    lax.fori_loop(0, n_chunks, unstage_lse, 0, unroll=4)


def _attn_prompt_group(tiles, g, batch):
    _, dil = DIL_GROUPS[g]
    tile = (None, SEQ, WIDTH)
    srcs = [tiles[g], tiles[J_K0 + g], tiles[J_V0 + g]]
    in_specs = [pl.BlockSpec(tile, lambda b, k=k: (k, b, 0)) for _, k in srcs]
    scratch = ([pltpu.VMEM((SEQ, STAGE_HEADS * HEAD_DIM), BF16)] * 4
               + [pltpu.VMEM((SEQ, HEADS * LSE_REP), F32)])
    return pl.pallas_call(
        functools.partial(_attn_prompt_body, dil=dil),
        grid=(batch,),
        in_specs=in_specs,
        out_specs=[pl.BlockSpec((SEQ, WIDTH), lambda b: (b, 0)),
                   pl.BlockSpec((SEQ, HEADS * LSE_REP), lambda b: (b, 0))],
        out_shape=[jax.ShapeDtypeStruct((batch * SEQ, WIDTH), BF16),
                   jax.ShapeDtypeStruct((batch * SEQ, HEADS * LSE_REP), F32)],
        scratch_shapes=scratch,
        compiler_params=pltpu.CompilerParams(
            dimension_semantics=("arbitrary",), vmem_limit_bytes=VMEM_LIMIT),
        name="attn_prompt_g%d" % g,
    )(*[arr for arr, _ in srcs])


def _attn_sample_body(qkv_ref, ck0_ref, cv0_ref, ck1_ref, cv1_ref, ck2_ref, cv2_ref,
                      pq_ref, pkp_ref, pkc_ref, pvp_ref, pvc_ref, o_ref,
                      nk0_ref, nk1_ref, nk2_ref, nv0_ref, nv1_ref, nv2_ref, po_ref, pl_ref,
                      bc0_ref, bc1_ref, bc2_ref, bn_ref, *, n_new):
    scale = np.float32(HEAD_DIM ** -0.5)
    q_refs, kn_refs, vn_refs = (
        tuple(qkv_ref.at[lo + g] for g in range(N_DIL)) for lo in (0, N_DIL, 2 * N_DIL))
    ck_refs = (ck0_ref, ck1_ref, ck2_ref)
    cv_refs = (cv0_ref, cv1_ref, cv2_ref)
    bc_refs = (bc0_ref, bc1_ref, bc2_ref)
    n_q = n_new * HEADS
    heads_log2 = HEADS.bit_length() - 1

    @pl.when(pl.program_id(0) == 0)
    def _():
        for g, (_, dil) in enumerate(DIL_GROUPS):
            n_res = min(dil, n_new)
            dil_log2 = dil.bit_length() - 1
            res_log2 = n_res.bit_length() - 1
            n_keys = BAND * n_res * HEADS
            row = lax.broadcasted_iota(jnp.int32, (n_q, n_keys), 0)
            col = lax.broadcasted_iota(jnp.int32, (n_q, n_keys), 1)
            t = row >> heads_log2
            ok = (((col & (HEADS - 1)) == (row & (HEADS - 1)))
                  & (((col >> heads_log2) & (n_res - 1)) == (t & (dil - 1)))
                  & ((col >> (heads_log2 + res_log2)) >= (t >> dil_log2)))
            bc_refs[g][...] = jnp.where(ok, 0.0, -jnp.inf).astype(F32)
            row = lax.broadcasted_iota(jnp.int32, (n_q, n_q), 0)
            col = lax.broadcasted_iota(jnp.int32, (n_q, n_q), 1)
            dn = (row >> heads_log2) - (col >> heads_log2)
            ok = (((col & (HEADS - 1)) == (row & (HEADS - 1)))
                  & (dn >= 0) & ((dn & (dil - 1)) == 0))
            bn_ref[g] = jnp.where(ok, 0.0, -jnp.inf).astype(F32)

    contract_last = (((1,), (1,)), ((), ()))
    outs, lses = [], []
    for g, (_, dil) in enumerate(DIL_GROUPS):
        n_keys = BAND * min(dil, n_new) * HEADS
        q = q_refs[g][...].reshape(n_q, HEAD_DIM).astype(BF16)
        kn = kn_refs[g][...].reshape(n_q, HEAD_DIM).astype(BF16)
        vn = vn_refs[g][...].reshape(n_q, HEAD_DIM).astype(BF16)
        kc = ck_refs[g][...].reshape(n_keys, HEAD_DIM).astype(BF16)
        vc = cv_refs[g][...].reshape(n_keys, HEAD_DIM).astype(BF16)
        s = lax.dot_general(q, kc, contract_last, preferred_element_type=F32) * scale
        s = s + bc_refs[g][...]
        s_new = lax.dot_general(q, kn, contract_last, preferred_element_type=F32) * scale
        s_new = s_new + bn_ref[g]
        m = jnp.maximum(jnp.max(s, axis=-1, keepdims=True),
                        jnp.max(s_new, axis=-1, keepdims=True))
        p = jnp.exp(s - m)
        p_new = jnp.exp(s_new - m)
        den = jnp.sum(p, axis=-1, keepdims=True) + jnp.sum(p_new, axis=-1, keepdims=True)
        acc = (jnp.dot(p.astype(BF16), vc, preferred_element_type=F32)
               + jnp.dot(p_new.astype(BF16), vn, preferred_element_type=F32))
        outs.append(acc / den)
        lses.append(m + jnp.log(den))
    mx = jnp.maximum(jnp.maximum(lses[0], lses[1]), lses[2])
    ws = [jnp.exp(l - mx) for l in lses]
    tot = ws[0] + ws[1] + ws[2]
    o = (ws[0] * outs[0] + ws[1] * outs[1] + ws[2] * outs[2]) / tot
    o_ref[...] = o.reshape(n_new, HEADS, HEAD_DIM)
    for dst, src in zip((nk0_ref, nk1_ref, nk2_ref, nv0_ref, nv1_ref, nv2_ref),
                        kn_refs + vn_refs):
        dst[...] = src[...]

    first_block = (pl.program_id(0) & (SEQ // BAND - 1)) == 0
    min_key = jnp.where(first_block, BAND, 0)
    head_lanes = [slice(h * HEAD_DIM, (h + 1) * HEAD_DIM) for h in range(HEADS)]
    res = _band_softmax(
        [(pq_ref[:, lanes],
          jnp.concatenate([pkp_ref[:, lanes], pkc_ref[:, lanes]], axis=0),
          jnp.concatenate([pvp_ref[:, lanes], pvc_ref[:, lanes]], axis=0))
         for lanes in head_lanes], BAND, min_key)
    for lanes, (o_blk, _) in zip(head_lanes, res):
        po_ref[:, lanes] = o_blk.astype(BF16)
    pl_ref[...] = _lse_tile([lse for _, lse in res])


def _attn_sample(qkv, caches_k, caches_v, tiles, n_batch, n_new):
    row_block = (n_new, HEADS, HEAD_DIM)
    in_specs = [pl.BlockSpec((3 * N_DIL,) + row_block, lambda b: (0, b, 0, 0))]
    args = [qkv]
    n_q = n_new * HEADS
    scratch = []
    for g, (win, dil) in enumerate(DIL_GROUPS):
        n_res = min(dil, n_new)
        for c in (caches_k[g], caches_v[g]):
            in_specs.append(pl.BlockSpec((None, BAND, n_res, HEADS, HEAD_DIM),
                                         lambda b: (b, 0, 0, 0, 0)))
            args.append(c.reshape(n_batch, win // dil, dil, HEADS, HEAD_DIM))
        scratch.append(pltpu.VMEM((n_q, BAND * n_res * HEADS), F32))
    scratch.append(pltpu.VMEM((N_DIL, n_q, n_q), F32))
    blocks_per_seq = SEQ // BAND
    (q_arr, q_idx), (k_arr, k_idx), (v_arr, v_idx) = tiles[0], tiles[J_K0], tiles[J_V0]
    n_prompt_rows = q_arr.shape[1]
    assert n_prompt_rows == n_batch * BAND, "one prompt query block per sample step"
    prev = lambda s: jnp.where((s & (blocks_per_seq - 1)) == 0, s, s - 1)
    tile = (None, BAND, WIDTH)
    in_specs += [pl.BlockSpec(tile, lambda s: (q_idx, s, 0)),
                 pl.BlockSpec(tile, lambda s: (k_idx, prev(s), 0)),
                 pl.BlockSpec(tile, lambda s: (k_idx, s, 0)),
                 pl.BlockSpec(tile, lambda s: (v_idx, prev(s), 0)),
                 pl.BlockSpec(tile, lambda s: (v_idx, s, 0))]
    args += [q_arr, k_arr, k_arr, v_arr, v_arr]
    row_spec = pl.BlockSpec(row_block, lambda b: (b, 0, 0))
    n_out = 1 + 2 * N_DIL
    body = functools.partial(_attn_sample_body, n_new=n_new)
    return pl.pallas_call(
        body,
        grid=(n_batch,),
        in_specs=in_specs,
        out_specs=[row_spec] * n_out + [pl.BlockSpec((BAND, WIDTH), lambda s: (s, 0)),
                                        pl.BlockSpec((BAND, HEADS * LSE_REP), lambda s: (s, 0))],
        out_shape=([jax.ShapeDtypeStruct((n_batch * n_new,) + row_block[1:], F32)] * n_out
                   + [jax.ShapeDtypeStruct((n_prompt_rows, WIDTH), BF16),
                      jax.ShapeDtypeStruct((n_prompt_rows, HEADS * LSE_REP), F32)]),
        scratch_shapes=scratch,
        compiler_params=pltpu.CompilerParams(
            dimension_semantics=("arbitrary",), vmem_limit_bytes=VMEM_LIMIT),
        name="attn_sample",
    )(*args)


def _final_body(*refs, tm, mix_rows, alpha, n_groups):
    zb_ref, gu_ref, vn_ref, za_ref, ga0_ref, ga1_ref, gb0_ref, gb1_ref = refs[:8]
    refs = refs[8:]
    if n_groups:
        og_refs = refs[:n_groups]
        lg_refs = refs[n_groups:2 * n_groups]
        refs = refs[2 * n_groups:]
    else:
        o_ref = refs[0]
        refs = refs[1:]
    (x_ref, wmix_ref, bmix_ref, woa_ref, wob_ref, wout_ref, lng_ref, lnb_ref,
     y_ref, ya_ref, yb_ref) = refs

    if n_groups:
        lses = [l_ref[...] for l_ref in lg_refs]
        mx = functools.reduce(jnp.maximum, lses)
        ws = [jnp.exp(l - mx) for l in lses]
        tot = functools.reduce(lambda a, b: a + b, ws)
        ws = [w / tot for w in ws]
        for h in range(HEADS):
            lanes = slice(h * HEAD_DIM, (h + 1) * HEAD_DIM)
            o = None
            for w, og_ref in zip(ws, og_refs):
                term = w[:, h * LSE_REP:h * LSE_REP + 1] * og_ref[:, lanes].astype(F32)
                o = term if o is None else o + term
            yb_ref[:, lanes] = (o * zb_ref[:, lanes].astype(F32)).astype(BF16)
    else:
        yb_ref[...] = (o_ref[...] * zb_ref[...].astype(F32)).astype(BF16)

    row = lax.broadcasted_iota(jnp.int32, (CHUNK, CHUNK), 0)
    col = lax.broadcasted_iota(jnp.int32, (CHUNK, CHUNK), 1)
    mix_log2 = mix_rows.bit_length() - 1
    causal = (row >= col) & ((row >> mix_log2) == (col >> mix_log2))
    for g in range(HEADS):
        lanes = slice(g * CHUNK, (g + 1) * CHUNK)
        w_c = jnp.where(causal, wmix_ref[g], 0.0).astype(BF16)
        for c in range(tm // CHUNK):
            rows = slice(c * CHUNK, (c + 1) * CHUNK)
            mixed = jnp.dot(w_c, vn_ref[rows, lanes], preferred_element_type=F32)
            mixed = mixed + bmix_ref[:, lanes]
            ya = gu_ref[rows, lanes].astype(F32) * mixed * za_ref[rows, lanes].astype(F32)
            ya_ref[rows, lanes] = ya.astype(BF16)

    proj_a = jnp.dot(ya_ref[...], woa_ref[...], preferred_element_type=F32)
    proj_b = jnp.dot(yb_ref[...], wob_ref[...], preferred_element_type=F32)
    ga = jnp.concatenate([ga0_ref[...], ga1_ref[...]], axis=1).astype(F32)
    gb = jnp.concatenate([gb0_ref[...], gb1_ref[...]], axis=1).astype(F32)
    merged = (ga * proj_a + gb * proj_b).astype(BF16)
    z = alpha * x_ref[...] + jnp.dot(merged, wout_ref[...], preferred_element_type=F32)
    y_ref[...] = _layer_norm_rows(z, lng_ref[...], lnb_ref[...])


def _final(tiles, o_parts, lse_parts, x, wmix, bmix, woa, wob, wout, ln_g, ln_b,
           mix_rows, alpha, name):
    n_rows = x.shape[0]
    tm = FINAL_ROWS
    n_groups = len(lse_parts)
    hspec = lambda j: pl.BlockSpec((None, tm, WIDTH), lambda i, k=tiles[j][1]: (k, i, 0))
    h_cols = (J_ZB, J_U, J_VA, J_ZA, J_GA, J_GA + 1, J_GB, J_GB + 1)
    rowspec = lambda width: pl.BlockSpec((tm, width), lambda i: (i, 0))
    const = lambda shape: pl.BlockSpec(shape, lambda i: (0,) * len(shape),
                                       pipeline_mode=pl.Buffered(1))
    in_specs = [hspec(j) for j in h_cols]
    in_specs += [rowspec(WIDTH)] * len(o_parts)
    in_specs += [rowspec(HEADS * LSE_REP)] * n_groups
    in_specs += [rowspec(D_MODEL),
                 const((HEADS, CHUNK, CHUNK)), const((CHUNK, WIDTH)),
                 const((WIDTH, D_MODEL)), const((WIDTH, D_MODEL)), const((D_MODEL, D_MODEL)),
                 const((1, D_MODEL)), const((1, D_MODEL))]
    body = functools.partial(_final_body, tm=tm, mix_rows=mix_rows, alpha=np.float32(alpha),
                             n_groups=n_groups)
    return pl.pallas_call(
        body,
        grid=(n_rows // tm,),
        in_specs=in_specs,
        out_specs=pl.BlockSpec((tm, D_MODEL), lambda i: (i, 0)),
        out_shape=jax.ShapeDtypeStruct((n_rows, D_MODEL), F32),
        scratch_shapes=[pltpu.VMEM((tm, WIDTH), BF16), pltpu.VMEM((tm, WIDTH), BF16)],
        compiler_params=pltpu.CompilerParams(
            dimension_semantics=("arbitrary",), vmem_limit_bytes=VMEM_LIMIT),
        name=name,
    )(*[tiles[j][0] for j in h_cols], *o_parts, *lse_parts,
      x, wmix, bmix, woa, wob, wout, ln_g, ln_b)


def _rope_tables(pos):
    half = HEAD_DIM // 2
    inv = ROPE_THETA ** (-jnp.arange(0, half, dtype=F32) * 2.0 / HEAD_DIM)
    ang = pos.astype(F32)[:, None] * inv[None, :]
    cos = jnp.cos(ang)
    sin = jnp.sin(ang)
    return jnp.concatenate([cos, cos], axis=-1), jnp.concatenate([-sin, sin], axis=-1)


def kernel(x_prompt, x_sample, cache_k_w128, cache_v_w128, cache_k_w512, cache_v_w512,
           cache_k_w2048, cache_v_w2048, w_in, b_in, w_s, b_s, ln_v_g, ln_v_b,
           w_o_a, w_o_b, w_out, ln_g, ln_b):
    depth = w_in.shape[0]
    assert depth == 1, "single-layer step"
    batch, seq, _ = x_prompt.shape
    n_dec, n_new, _ = x_sample.shape
    assert seq == SEQ and n_new == 8 and x_prompt.shape[2] == D_MODEL
    caches_k = (cache_k_w128, cache_k_w512, cache_k_w2048)
    caches_v = (cache_v_w128, cache_v_w512, cache_v_w2048)
    for g, (win, _) in enumerate(DIL_GROUPS):
        assert caches_k[g].shape == (depth, n_dec, win, HEADS, HEAD_DIM)
    past_len = cache_k_w2048.shape[2]
    alpha = float(2 * depth) ** 0.25

    w_f32 = w_in.reshape(D_MODEL, -1)
    b2 = b_in.reshape(1, -1)
    lvg = ln_v_g.reshape(1, WIDTH)
    lvb = ln_v_b.reshape(1, WIDTH)
    lg = ln_g.reshape(1, D_MODEL)
    lb = ln_b.reshape(1, D_MODEL)
    ws = w_s.reshape(HEADS, CHUNK, CHUNK)
    bs = b_s.reshape(HEADS, CHUNK)
    woa = w_o_a.reshape(WIDTH, D_MODEL).astype(BF16)
    wob = w_o_b.reshape(WIDTH, D_MODEL).astype(BF16)
    wout = w_out.reshape(D_MODEL, D_MODEL).astype(BF16)

    cos_p, sin_p = _rope_tables(jnp.arange(seq, dtype=jnp.int32))
    pos_s = past_len + jnp.tile(jnp.arange(n_new, dtype=jnp.int32), n_dec)
    cos_s, sin_s = _rope_tables(pos_s)

    xp2 = x_prompt.reshape(batch * seq, D_MODEL)
    xs2 = x_sample.reshape(n_dec * n_new, D_MODEL)

    tiles_p, new_pk, new_pv = _inproj_prompt(xp2.astype(BF16), w_f32, b2, cos_p, sin_p,
                                             lvg, lvb)
    hs3d, qkv_s, gv_s = _inproj_sample(xs2.astype(BF16), w_f32, b2, cos_s, sin_s, lvg, lvb)
    qkv_s = qkv_s.reshape(3 * N_DIL, n_dec * n_new, HEADS, HEAD_DIM)

    aouts = _attn_sample(qkv_s, caches_k, caches_v, tiles_p, n_dec, n_new)
    o_s, kn_s, vn_s = aouts[0], aouts[1:1 + N_DIL], aouts[1 + N_DIL:1 + 2 * N_DIL]
    o_parts, lse_parts = [aouts[-2]], [aouts[-1]]
    for g in range(1, N_DIL):
        o_g, lse_g = _attn_prompt_group(tiles_p, g, batch)
        o_parts.append(o_g)
        lse_parts.append(lse_g)

    bmix_p = jnp.repeat(bs.T, CHUNK, axis=1)
    yp = _final(tiles_p, o_parts, lse_parts, xp2, ws, bmix_p, woa, wob, wout, lg, lb,
                CHUNK, alpha, "final_prompt")
    reps = CHUNK // n_new
    wmix_s = jnp.tile(ws[:, :n_new, :n_new], (1, reps, reps))
    bmix_s = jnp.repeat(jnp.tile(bs[:, :n_new], (1, reps)).T, CHUNK, axis=1)
    tiles_s = {c: (hs3d, c) for c in ALL_COLS}
    ys = _final(tiles_s, [o_s.reshape(n_dec * n_new, WIDTH)], [], xs2, wmix_s, bmix_s,
                woa, wob, wout, lg, lb, n_new, alpha, "final_sample")

    shp_p = lambda a: a.reshape(depth, batch, -1, HEADS, HEAD_DIM)
    new_p = tuple(shp_p(a) for pair in zip(new_pk, new_pv) for a in pair)
    shp_s = lambda a: a.reshape(depth, n_dec, n_new, HEADS, HEAD_DIM)
    new_s = tuple(shp_s(a) for pair in zip(kn_s, vn_s) for a in pair)
    return ((yp.reshape(batch, seq, D_MODEL), ys.reshape(n_dec, n_new, D_MODEL))
            + new_p + new_s + (gv_s.reshape(depth, n_dec, n_new, WIDTH),))
```
